```python
import math
import jax, jax.numpy as jnp
from jax import lax
import numpy as np

D_MODEL = 2048
BATCH = 4
SEQ = 2048
DEPTH = 4
DEC_BATCH = 8
DEC_SEQ = 8
PAST_LEN = 16384
PAGE_SIZE = 128

D_MIX = D_MODEL
D_RG = D_MIX // 2
D_ATT = D_MIX - D_RG
H_RG = 16
RG_BW = D_RG // H_RG
CONV_W = 4
LRU_C = 8.0
DA_HEAD = 64
H_DA = D_ATT // (2 * DA_HEAD)
DA_VDIM = 2 * DA_HEAD
Q_BLOCK = 128
ALPHA = (2 * DEPTH) ** 0.25
BETA = (8 * DEPTH) ** -0.25
LN_EPS = 1e-5
SUBLN_EPS = 1e-5
N_IN = 2 * D_RG + 4 * D_ATT
SPLITS = [D_RG, 2 * D_RG, 2 * D_RG + D_ATT, 2 * D_RG + 2 * D_ATT, 2 * D_RG + 3 * D_ATT]

kernel_name = "hymba_rglru_diffattn_deepnorm_step"


def _lambda_init(layer):
    return 0.8 - 0.6 * math.exp(-0.3 * layer)


def _alibi_slopes():
    return jnp.exp2(-8.0 * jnp.arange(1, H_DA + 1, dtype=jnp.float32) / H_DA)


def _layernorm(x, g, b):
    xf = x.astype(jnp.float32)
    mu = jnp.mean(xf, -1, keepdims=True)
    var = jnp.mean(jnp.square(xf - mu), -1, keepdims=True)
    return ((xf - mu) * lax.rsqrt(var + LN_EPS) * g.astype(jnp.float32) + b.astype(jnp.float32)).astype(x.dtype)


def _rmsnorm(x, g):
    xf = x.astype(jnp.float32)
    ms = jnp.mean(jnp.square(xf), -1, keepdims=True)
    return (xf * lax.rsqrt(ms + SUBLN_EPS) * g.astype(jnp.float32)).astype(x.dtype)


def _causal_conv(u, buf, w, b):
    T = u.shape[1]
    xp = jnp.concatenate([buf.astype(u.dtype), u], axis=1)
    y = b + sum(xp[:, j:j + T] * w[j] for j in range(CONV_W))
    return y, xp[:, -(CONV_W - 1):]


def _rg_lru(u, h0, w_a, b_a, w_x, b_x, lam):
    B, T, _ = u.shape
    ub = u.reshape(B, T, H_RG, RG_BW)
    r = jax.nn.sigmoid(jnp.einsum('btnc,ncd->btnd', ub, w_a).reshape(B, T, D_RG) + b_a)
    i = jax.nn.sigmoid(jnp.einsum('btnc,ncd->btnd', ub, w_x).reshape(B, T, D_RG) + b_x)
    log_a = (-LRU_C * r.astype(jnp.float32)) * jax.nn.softplus(-lam.astype(jnp.float32))
    a = jnp.exp(log_a)
    bterm = jnp.sqrt(-jnp.expm1(2.0 * log_a)) * (i * u).astype(jnp.float32)

    def step(h, ab):
        a_t, b_t = ab
        h = a_t * h + b_t
        return h, h

    hT, hs = lax.scan(step, h0.astype(jnp.float32), (a.swapaxes(0, 1), bterm.swapaxes(0, 1)))
    return hs.swapaxes(0, 1).astype(u.dtype), hT


def _diff_logits(q, k, q_pos, k_pos, slopes):
    logits = jnp.einsum('bthcd,bshcd->bhcts', q, k).astype(jnp.float32) * (DA_HEAD ** -0.5)
    dist = (q_pos[:, None] - k_pos[None, :]).astype(jnp.float32)
    logits = logits - slopes[None, :, None, None, None] * dist
    return jnp.where(k_pos[None, :] <= q_pos[:, None], logits, -jnp.inf)


def _attn_prompt(q, k, v, lam, slopes):
    B, S = q.shape[:2]
    nb = S // Q_BLOCK
    qb = q.reshape(B, nb, Q_BLOCK, H_DA, 2, DA_HEAD).swapaxes(0, 1)
    k_pos = jnp.arange(S)

    def block(args):
        q_blk, start = args
        q_pos = start + jnp.arange(Q_BLOCK)
        p = jax.nn.softmax(_diff_logits(q_blk, k, q_pos, k_pos, slopes), axis=-1)
        w = (p[:, :, 0] - lam * p[:, :, 1]).astype(v.dtype)
        return jnp.einsum('bhts,bshe->bthe', w, v)

    out = lax.map(block, (qb, jnp.arange(nb) * Q_BLOCK))
    return out.swapaxes(0, 1).reshape(B, S, H_DA, DA_VDIM)


def _attn_sample(q, k_new, v_new, k_past, v_past, lam, slopes):
    T = q.shape[1]
    P = k_past.shape[1]
    q_pos = P + jnp.arange(T)
    lp = _diff_logits(q, k_past, q_pos, jnp.arange(P), slopes)
    ln = _diff_logits(q, k_new, q_pos, P + jnp.arange(T), slopes)
    p = jax.nn.softmax(jnp.concatenate([lp, ln], axis=-1), axis=-1)
    w = (p[:, :, 0] - lam * p[:, :, 1]).astype(v_new.dtype)
    return (jnp.einsum('bhts,bshe->bthe', w[..., :P], v_past)
            + jnp.einsum('bhts,bshe->bthe', w[..., P:], v_new))


def _layer(x, conv_buf, h0, kv_past, layer, w_in, conv_w, conv_b, w_ga, b_ga, w_gx, b_gx, lru_lam,
           lq1, lk1, lq2, lk2, subln_g, w_out, ln_g, ln_b):
    B, T, _ = x.shape
    proj = jnp.einsum('btd,de->bte', x, w_in)
    u, g_rg, q, k, v, g_da = jnp.split(proj, SPLITS, axis=-1)
    u_conv, new_buf = _causal_conv(u, conv_buf, conv_w, conv_b)
    y_rg, hT = _rg_lru(u_conv, h0, w_ga, b_ga, w_gx, b_gx, lru_lam)
    q = q.reshape(B, T, H_DA, 2, DA_HEAD)
    k = k.reshape(B, T, H_DA, 2, DA_HEAD)
    v = v.reshape(B, T, H_DA, DA_VDIM)
    lam_init = _lambda_init(layer)
    lam = (jnp.exp(jnp.sum(lq1.astype(jnp.float32) * lk1.astype(jnp.float32)))
           - jnp.exp(jnp.sum(lq2.astype(jnp.float32) * lk2.astype(jnp.float32))) + lam_init)
    slopes = _alibi_slopes()
    if kv_past is None:
        o = _attn_prompt(q, k, v, lam, slopes)
    else:
        o = _attn_sample(q, k, v, kv_past[0], kv_past[1], lam, slopes)
    o = _rmsnorm(o, subln_g) * (1.0 - lam_init)
    mixed = jnp.concatenate([y_rg * jax.nn.silu(g_rg), o.reshape(B, T, D_ATT) * jax.nn.silu(g_da)], axis=-1)
    y = jnp.einsum('bte,ed->btd', mixed, w_out)
    x_new = _layernorm(ALPHA * x + y, ln_g, ln_b)
    return x_new, new_buf, hT, k.reshape(B, T, H_DA, 2 * DA_HEAD), v


def setup_inputs(seed: int = 0) -> dict:
    key = jax.random.key(seed)
    ks = jax.random.split(key, 24)
    n_pages = PAST_LEN // PAGE_SIZE
    n_used = DEC_BATCH * n_pages
    n_pool = n_used + max(1, n_used // 4)
    f32 = jnp.float32
    x_prompt = jax.random.normal(ks[0], (BATCH, SEQ, D_MODEL), f32)
    x_sample = jax.random.normal(ks[1], (DEC_BATCH, DEC_SEQ, D_MODEL), f32)
    cache_k = jax.random.normal(ks[2], (DEPTH, n_pool, PAGE_SIZE, H_DA, 2 * DA_HEAD), f32)
    cache_v = jax.random.normal(ks[3], (DEPTH, n_pool, PAGE_SIZE, H_DA, DA_VDIM), f32) * BETA
    state_h = jax.random.normal(ks[4], (DEPTH, DEC_BATCH, D_RG), f32) * 0.5
    state_conv = jax.random.normal(ks[5], (DEPTH, DEC_BATCH, CONV_W - 1, D_RG), f32) * 0.5
    page_table = jax.random.permutation(ks[6], n_pool)[:n_used].reshape(DEC_BATCH, n_pages).astype(jnp.int32)
    w_in = jax.random.normal(ks[7], (DEPTH, D_MODEL, N_IN), f32) * D_MODEL ** -0.5
    col_scale = jnp.ones((N_IN,), f32).at[SPLITS[3]:SPLITS[4]].set(BETA)
    w_in = w_in * col_scale
    conv_w = jax.random.normal(ks[8], (DEPTH, CONV_W, D_RG), f32) * CONV_W ** -0.5
    conv_b = jax.random.normal(ks[9], (DEPTH, D_RG), f32) * 0.02
    w_ga = jax.random.normal(ks[10], (DEPTH, H_RG, RG_BW, RG_BW), f32) * RG_BW ** -0.5
    b_ga = jax.random.normal(ks[11], (DEPTH, D_RG), f32) * 0.1
    w_gx = jax.random.normal(ks[12], (DEPTH, H_RG, RG_BW, RG_BW), f32) * RG_BW ** -0.5
    b_gx = jax.random.normal(ks[13], (DEPTH, D_RG), f32) * 0.1
    a0 = jax.random.uniform(ks[14], (DEPTH, D_RG), f32, 0.9, 0.999)
    lru_lam = jnp.log(a0) - jnp.log1p(-a0)
    lq1 = jax.random.normal(ks[15], (DEPTH, DA_HEAD), f32) * 0.1
    lk1 = jax.random.normal(ks[16], (DEPTH, DA_HEAD), f32) * 0.1
    lq2 = jax.random.normal(ks[17], (DEPTH, DA_HEAD), f32) * 0.1
    lk2 = jax.random.normal(ks[18], (DEPTH, DA_HEAD), f32) * 0.1
    subln_g = 1.0 + 0.02 * jax.random.normal(ks[19], (DEPTH, DA_VDIM), f32)
    w_out = jax.random.normal(ks[20], (DEPTH, D_MIX, D_MODEL), f32) * (D_MIX ** -0.5) * BETA
    ln_g = 1.0 + 0.02 * jax.random.normal(ks[21], (DEPTH, D_MODEL), f32)
    ln_b = 0.02 * jax.random.normal(ks[22], (DEPTH, D_MODEL), f32)
    return {"x_prompt": x_prompt, "x_sample": x_sample, "cache_k": cache_k, "cache_v": cache_v,
            "state_h": state_h, "state_conv": state_conv, "page_table": page_table,
            "w_in": w_in, "conv_w": conv_w, "conv_b": conv_b, "w_ga": w_ga, "b_ga": b_ga,
            "w_gx": w_gx, "b_gx": b_gx, "lru_lam": lru_lam, "lq1": lq1, "lk1": lk1,
            "lq2": lq2, "lk2": lk2, "subln_g": subln_g, "w_out": w_out, "ln_g": ln_g, "ln_b": ln_b}


def reference(x_prompt, x_sample, cache_k, cache_v, state_h, state_conv, page_table,
              w_in, conv_w, conv_b, w_ga, b_ga, w_gx, b_gx, lru_lam, lq1, lk1, lq2, lk2,
              subln_g, w_out, ln_g, ln_b):
    Bp = x_prompt.shape[0]
    Bs = x_sample.shape[0]
    past = page_table.shape[1] * cache_k.shape[2]
    xp, xs = x_prompt, x_sample
    kp_l, vp_l, hp_l, cp_l, ks_l, vs_l, hs_l, cs_l = [], [], [], [], [], [], [], []
    for l in range(DEPTH):
        w = (w_in[l], conv_w[l], conv_b[l], w_ga[l], b_ga[l], w_gx[l], b_gx[l], lru_lam[l],
             lq1[l], lk1[l], lq2[l], lk2[l], subln_g[l], w_out[l], ln_g[l], ln_b[l])
        buf0 = jnp.zeros((Bp, CONV_W - 1, D_RG), xp.dtype)
        h0 = jnp.zeros((Bp, D_RG), jnp.float32)
        xp, cbp, hTp, kp, vp = _layer(xp, buf0, h0, None, l, *w)
        k_past = cache_k[l, page_table].reshape(Bs, past, H_DA, 2, DA_HEAD)
        v_past = cache_v[l, page_table].reshape(Bs, past, H_DA, DA_VDIM)
        xs, cbs, hTs, ksn, vsn = _layer(xs, state_conv[l], state_h[l], (k_past, v_past), l, *w)
        kp_l.append(kp); vp_l.append(vp); hp_l.append(hTp); cp_l.append(cbp)
        ks_l.append(ksn); vs_l.append(vsn); hs_l.append(hTs); cs_l.append(cbs)
    return (xp, xs,
            jnp.stack(kp_l), jnp.stack(vp_l), jnp.stack(hp_l), jnp.stack(cp_l),
            jnp.stack(ks_l), jnp.stack(vs_l), jnp.stack(hs_l), jnp.stack(cs_l))
```

```python
import functools
import math

import jax
import jax.numpy as jnp
from jax import lax
from jax.experimental import pallas as pl
from jax.experimental.pallas import tpu as pltpu

F32 = jnp.float32
BF16 = jnp.bfloat16

H_RG = 16
CONV_W = 4
LRU_C = 8.0
DA_HEAD = 64
DA_VDIM = 2 * DA_HEAD
LN_EPS = 1e-5
SUBLN_EPS = 1e-5
MASKED = -1e30

LANES = 128
SUBLANES = 8
GATE_TILE = 256
V7X_VMEM_BYTES = 64 * 1024 * 1024


def _vmem_limit(nbytes):
    return int(min(V7X_VMEM_BYTES - (8 << 20), nbytes + (16 << 20)))


def _lambda_init(layer):
    return 0.8 - 0.6 * math.exp(-0.3 * layer)


def _diff_lambda(lq1_ref, lk1_ref, lq2_ref, lk2_ref, lam_init):
    a = jnp.sum(lq1_ref[...] * lk1_ref[...], axis=1, keepdims=True)
    b = jnp.sum(lq2_ref[...] * lk2_ref[...], axis=1, keepdims=True)
    return jnp.exp(a) - jnp.exp(b) + lam_init


def _silu(x):
    return x * jax.nn.sigmoid(x)


def _dot_nt(a, b):
    return lax.dot_general(a, b, (((1,), (1,)), ((), ())), preferred_element_type=F32)


def _proj_body(*refs, mode, aliased):
    xp_ref, xs_ref, w_ref = refs[:3]
    outs = refs[3 + (1 if aliased else 0):]
    w = w_ref[...]
    acc = jnp.dot(xp_ref[...], w, preferred_element_type=F32)
    if mode == "f32":
        op_ref, os_ref = outs
        op_ref[...] = acc
    elif mode == "bf16":
        op_ref, os_ref = outs
        op_ref[...] = acc.astype(BF16)
    else:
        op_ref, o3_ref, os_ref = outs
        op_ref[...] = acc.astype(BF16)
        for h in range(o3_ref.shape[1]):
            o3_ref[:, h, :] = acc[:, h * LANES:(h + 1) * LANES]

    @pl.when(pl.program_id(1) == 0)
    def _():
        os_ref[...] = jnp.dot(xs_ref[...], w, preferred_element_type=F32)


def _proj(xp, xs, w, *, col_blocks, mode, big=None, layer=0, depth=1, bm=1024, bn=1024):
    n_p, d = xp.shape
    n_s = xs.shape[0]
    ncol = len(col_blocks)
    if col_blocks == (0, 1, 5):
        wcol = lambda j: j + 3 * (j // 2)
    else:
        assert ncol == 1
        c0 = col_blocks[0]
        wcol = lambda j: c0
    in_specs = [
        pl.BlockSpec((bm, d), lambda j, i: (i, 0)),
        pl.BlockSpec((n_s, d), lambda j, i: (0, 0)),
        pl.BlockSpec((d, bn), lambda j, i: (0, wcol(j))),
    ]
    args = [xp, xs, w]
    aliases = {}
    nbytes = 2 * (bm * d * 2 + n_s * d * 2 + d * bn * 2 + n_s * bn * 4)
    if mode == "kv":
        heads = bn // LANES
        out_shape = [jax.ShapeDtypeStruct((n_p, bn), BF16),
                     jax.ShapeDtypeStruct((depth, n_p, heads, LANES), F32),
                     jax.ShapeDtypeStruct((n_s, bn), F32)]
        out_specs = [pl.BlockSpec((bm, bn), lambda j, i: (i, 0)),
                     pl.BlockSpec((None, bm, heads, LANES), lambda j, i: (layer, i, 0, 0)),
                     pl.BlockSpec((n_s, bn), lambda j, i: (0, 0))]
        nbytes += 2 * (bm * bn * 2 + bm * bn * 4)
        if big is not None:
            in_specs.append(pl.BlockSpec(memory_space=pl.ANY))
            args.append(big)
            aliases = {3: 1}
    else:
        odt = F32 if mode == "f32" else BF16
        out_shape = [jax.ShapeDtypeStruct((n_p, ncol * bn), odt),
                     jax.ShapeDtypeStruct((n_s, ncol * bn), F32)]
        out_specs = [pl.BlockSpec((bm, bn), lambda j, i: (i, j)),
                     pl.BlockSpec((n_s, bn), lambda j, i: (0, j))]
        nbytes += 2 * bm * bn * 4
    return pl.pallas_call(
        functools.partial(_proj_body, mode=mode, aliased=big is not None),
        grid=(ncol, n_p // bm),
        in_specs=in_specs,
        out_specs=out_specs,
        out_shape=out_shape,
        input_output_aliases=aliases,
        compiler_params=pltpu.CompilerParams(
            dimension_semantics=("arbitrary", "arbitrary"),
            vmem_limit_bytes=_vmem_limit(nbytes + bm * bn * 4)),
        name="proj_" + mode,
    )(*args)


def _rglru_body(u_ref, g_ref, cw_ref, cb_ref, wg_ref, ba_ref, bx_ref, lam_ref, h0_ref, c0_ref,
                o_ref, ht_ref, ct_ref, ext_scr, a_scr, b_scr, hs_scr, car_scr):
    bt, c = u_ref.shape
    t_blk = pl.program_id(1)
    halo = CONV_W - 1

    @pl.when(t_blk == 0)
    def _():
        ext_scr[SUBLANES - halo:SUBLANES, :] = c0_ref[...]
        car_scr[...] = jnp.broadcast_to(h0_ref[...], car_scr.shape)

    u = u_ref[...]
    ext_scr[SUBLANES:SUBLANES + bt, :] = u
    uc = cb_ref[...] + cw_ref[halo:halo + 1, :] * u
    for j in range(halo):
        lag = halo - j
        uc = uc + cw_ref[j:j + 1, :] * ext_scr[SUBLANES - lag:SUBLANES - lag + bt, :]
    tail = ext_scr[SUBLANES + bt - halo:SUBLANES + bt, :]
    ct_ref[...] = tail
    ext_scr[SUBLANES - halo:SUBLANES, :] = tail

    ucb = uc.astype(BF16)
    ngrp = c // GATE_TILE
    pre_a, pre_x = [], []
    for gi in range(ngrp):
        pre = jnp.dot(ucb[:, gi * GATE_TILE:(gi + 1) * GATE_TILE], wg_ref[gi],
                      preferred_element_type=F32)
        pre_a.append(pre[:, :GATE_TILE])
        pre_x.append(pre[:, GATE_TILE:])
    r = jax.nn.sigmoid(jnp.concatenate(pre_a, axis=1) + ba_ref[...])
    gate_i = jax.nn.sigmoid(jnp.concatenate(pre_x, axis=1) + bx_ref[...])
    z = -lam_ref[...]
    softplus = jnp.maximum(z, 0.0) + jnp.log1p(jnp.exp(-jnp.abs(z)))
    log_a = (-LRU_C * r) * softplus
    a = jnp.exp(log_a)
    one_minus_a2 = -jnp.tanh(log_a) * (a * a + 1.0)
    a_scr[...] = a
    b_scr[...] = jnp.sqrt(one_minus_a2) * (gate_i * uc)

    row = lax.broadcasted_iota(jnp.int32, (SUBLANES, c), 0)

    def scan_tile(k, carry):
        sl = pl.ds(pl.multiple_of(k * SUBLANES, SUBLANES), SUBLANES)
        av = a_scr[sl, :]
        bv = b_scr[sl, :]
        for s in (1, 2, 4):
            a_sh = jnp.where(row >= s, pltpu.roll(av, s, 0), 1.0)
            b_sh = jnp.where(row >= s, pltpu.roll(bv, s, 0), 0.0)
            bv = bv + av * b_sh
            av = av * a_sh
        h = bv + av * carry
        hs_scr[sl, :] = h
        return jnp.broadcast_to(h[SUBLANES - 1:SUBLANES, :], h.shape)

    carry = lax.fori_loop(0, bt // SUBLANES, scan_tile, car_scr[...])
    car_scr[...] = carry
    ht_ref[...] = carry[0:1, :]
    o_ref[...] = (hs_scr[...] * _silu(g_ref[...])).astype(o_ref.dtype)


def _rglru(ug, cw, cb, wg, ba, bx, lam, h0, c0, *, batch, seq, bt, out_dtype):
    c = cw.shape[1]
    nt = seq // bt
    halo = CONV_W - 1
    row_blk = lambda b, t: (b * nt + t, 0)
    const2 = lambda b, t: (0, 0)
    in_specs = [
        pl.BlockSpec((bt, c), row_blk),
        pl.BlockSpec((bt, c), lambda b, t: (b * nt + t, 1)),
        pl.BlockSpec(cw.shape, const2),
        pl.BlockSpec(cb.shape, const2),
        pl.BlockSpec(wg.shape, lambda b, t: (0, 0, 0)),
        pl.BlockSpec(ba.shape, const2),
        pl.BlockSpec(bx.shape, const2),
        pl.BlockSpec(lam.shape, const2),
        pl.BlockSpec((None, 1, c), lambda b, t: (b, 0, 0)),
        pl.BlockSpec((None, halo, c), lambda b, t: (b, 0, 0)),
    ]
    out_shape = [jax.ShapeDtypeStruct((batch * seq, c), out_dtype),
                 jax.ShapeDtypeStruct((batch, 1, c), F32),
                 jax.ShapeDtypeStruct((batch, halo, c), F32)]
    out_specs = [pl.BlockSpec((bt, c), row_blk),
                 pl.BlockSpec((None, 1, c), lambda b, t: (b, 0, 0)),
                 pl.BlockSpec((None, halo, c), lambda b, t: (b, 0, 0))]
    scratch = [pltpu.VMEM((bt + SUBLANES, c), F32), pltpu.VMEM((bt, c), F32),
               pltpu.VMEM((bt, c), F32), pltpu.VMEM((bt, c), F32),
               pltpu.VMEM((SUBLANES, c), F32)]
    nbytes = 2 * (2 * bt * c * 4 + bt * c * 4) + 4 * bt * c * 4 + wg.size * 2 * 2
    return pl.pallas_call(
        _rglru_body,
        grid=(batch, nt),
        in_specs=in_specs,
        out_specs=out_specs,
        out_shape=out_shape,
        scratch_shapes=scratch,
        compiler_params=pltpu.CompilerParams(
            dimension_semantics=("arbitrary", "arbitrary"),
            vmem_limit_bytes=_vmem_limit(nbytes + 8 * bt * c * 4)),
        name="rglru",
    )(ug, ug, cw, cb, wg, ba, bx, lam, h0, c0)


def _subln_gate(o, sg_ref, gate, lam_init):
    ms = jnp.mean(o * o, axis=-1, keepdims=True)
    o = o * lax.rsqrt(ms + SUBLN_EPS) * sg_ref[...]
    return (o * (1.0 - lam_init)) * _silu(gate)


def _attn_body(slope_ref, q_ref, k_ref, v_ref, g_ref, sg_ref, lq1_ref, lk1_ref, lq2_ref, lk2_ref,
               o_ref, m_scr, l_scr, acc_scr, *, lam_init):
    bq = q_ref.shape[0]
    bk = k_ref.shape[0]
    h = pl.program_id(1)
    qi = pl.program_id(2)
    ki = pl.program_id(3)

    @pl.when(ki == 0)
    def _():
        m_scr[...] = jnp.full(m_scr.shape, MASKED, F32)
        l_scr[...] = jnp.zeros(l_scr.shape, F32)
        acc_scr[...] = jnp.zeros(acc_scr.shape, F32)

    def update(diagonal):
        q = q_ref[...]
        lane = lax.broadcasted_iota(jnp.int32, q.shape, 1)
        zero = jnp.zeros_like(q)
        k = k_ref[...]
        v = v_ref[...]
        col = lax.broadcasted_iota(jnp.int32, (1, bk), 1)
        colbias = slope_ref[h] * (col + (ki * bk - qi * bq)).astype(F32)
        if diagonal:
            rows = lax.broadcasted_iota(jnp.int32, (bq, bk), 0)
            cols = lax.broadcasted_iota(jnp.int32, (bq, bk), 1)
            keep = cols <= rows
        for c in range(2):
            qc = jnp.where(lane < DA_HEAD, q, zero) if c == 0 else jnp.where(lane >= DA_HEAD, q, zero)
            x = _dot_nt(qc, k) * (DA_HEAD ** -0.5) + colbias
            if diagonal:
                x = jnp.where(keep, x, MASKED)
            m_old = m_scr[c]
            m_new = jnp.maximum(m_old, jnp.max(x, axis=1, keepdims=True))
            alpha = jnp.exp(m_old - m_new)
            p = jnp.exp(x - m_new)
            l_scr[c] = alpha * l_scr[c] + jnp.sum(p, axis=1, keepdims=True)
            acc_scr[c] = alpha * acc_scr[c] + jnp.dot(p.astype(BF16), v, preferred_element_type=F32)
            m_scr[c] = m_new

    @pl.when(ki < qi)
    def _():
        update(False)

    @pl.when(ki == qi)
    def _():
        update(True)
        lam = _diff_lambda(lq1_ref, lk1_ref, lq2_ref, lk2_ref, lam_init)
        o = acc_scr[0] / l_scr[0] - lam * (acc_scr[1] / l_scr[1])
        o_ref[...] = _subln_gate(o, sg_ref, g_ref[...], lam_init).astype(o_ref.dtype)


def _attn_prompt(slopes, q, k, v, ug, sg, lq1, lk1, lq2, lk2, *, batch, seq, heads, lam_init, blk=512):
    nq = seq // blk
    gate_col0 = 2 * (q.shape[1] // LANES)
    qmap = lambda b, h, qi, ki: (b * nq + qi, h)
    kmap = lambda b, h, qi, ki: (b * nq + jnp.minimum(ki, qi), h)
    small = lambda b, h, qi, ki: (0, 0)
    in_specs = [
        pl.BlockSpec(memory_space=pltpu.SMEM),
        pl.BlockSpec((blk, LANES), qmap),
        pl.BlockSpec((blk, LANES), kmap),
        pl.BlockSpec((blk, LANES), kmap),
        pl.BlockSpec((blk, LANES), lambda b, h, qi, ki: (b * nq + qi, gate_col0 + h)),
        pl.BlockSpec(sg.shape, small),
        pl.BlockSpec(lq1.shape, small), pl.BlockSpec(lk1.shape, small),
        pl.BlockSpec(lq2.shape, small), pl.BlockSpec(lk2.shape, small),
    ]
    return pl.pallas_call(
        functools.partial(_attn_body, lam_init=lam_init),
        grid=(batch, heads, nq, nq),
        in_specs=in_specs,
        out_specs=pl.BlockSpec((blk, LANES), qmap),
        out_shape=jax.ShapeDtypeStruct(q.shape, BF16),
        scratch_shapes=[pltpu.VMEM((2, blk, 1), F32), pltpu.VMEM((2, blk, 1), F32),
                        pltpu.VMEM((2, blk, DA_VDIM), F32)],
        compiler_params=pltpu.CompilerParams(
            dimension_semantics=("arbitrary", "arbitrary", "arbitrary", "arbitrary"),
            vmem_limit_bytes=_vmem_limit(16 * blk * blk * 4)),
        name="attn_prompt",
    )(slopes, q, k, v, ug, sg, lq1, lk1, lq2, lk2)


def _online_update(x, row_off, v_bf, m, l, acc):
    m_new = jnp.maximum(m, jnp.max(x, axis=1, keepdims=True) + row_off)
    alpha = jnp.exp(m - m_new)
    p = jnp.exp(x - (m_new - row_off))
    l = alpha * l + jnp.sum(p, axis=1, keepdims=True)
    acc = alpha * acc + jnp.dot(p.astype(BF16), v_bf, preferred_element_type=F32)
    return m_new, l, acc


def _decode_body(pt_ref, q_ref, kn_ref, vn_ref, g_ref, sg_ref, lq1_ref, lk1_ref, lq2_ref, lk2_ref,
                 *rest, pages_per_step, past_len, lam_init):
    del pt_ref
    gp = pages_per_step
    k_refs = rest[:gp]
    v_refs = rest[gp:2 * gp]
    o_ref = rest[2 * gp]
    w_scr, c_scr, m_scr, l_scr, acc_scr = rest[2 * gp + 1:]
    page, heads, _ = k_refs[0].shape
    t_new = q_ref.shape[0]
    rows = heads * 2 * t_new
    cols = page * heads
    b = pl.program_id(0)
    j = pl.program_id(1)
    nj = pl.num_programs(1)

    def row_slope(shape):
        hq = lax.broadcasted_iota(jnp.int32, shape, 0) // (2 * t_new)
        return jnp.exp2(-(hq + 1).astype(F32))

    def bias(ncols, causal):
        r = lax.broadcasted_iota(jnp.int32, (rows, ncols), 0)
        c = lax.broadcasted_iota(jnp.int32, (rows, ncols), 1)
        hq, t = r // (2 * t_new), r % t_new
        s, hk = c // heads, c % heads
        keep = hq == hk
        if causal:
            keep = jnp.logical_and(keep, s <= t)
        return jnp.where(keep, row_slope((rows, ncols)) * (s - t).astype(F32), MASKED)

    @pl.when(jnp.logical_and(b == 0, j == 0))
    def _():
        c_scr[...] = bias(cols, False)

    @pl.when(j == 0)
    def _():
        q = q_ref[...]
        lane = lax.broadcasted_iota(jnp.int32, (t_new, LANES), 1)
        parts = []
        for h in range(heads):
            qh = q[:, h * LANES:(h + 1) * LANES] * (DA_HEAD ** -0.5)
            parts.append(jnp.where(lane < DA_HEAD, qh, 0.0))
            parts.append(jnp.where(lane >= DA_HEAD, qh, 0.0))
        w_scr[...] = jnp.concatenate(parts, axis=0).astype(BF16)
        m_scr[...] = jnp.full(m_scr.shape, MASKED, F32)
        l_scr[...] = jnp.zeros(l_scr.shape, F32)
        acc_scr[...] = jnp.zeros(acc_scr.shape, F32)

    w = w_scr[...]
    slope_col = row_slope((rows, 1))
    m, l, acc = m_scr[...], l_scr[...], acc_scr[...]
    for gi in range(gp):
        kp = k_refs[gi][...].reshape(cols, LANES).astype(BF16)
        vp = v_refs[gi][...].reshape(cols, LANES).astype(BF16)
        x = _dot_nt(w, kp) + c_scr[...]
        page_pos = (j * gp + gi) * page - past_len
        m, l, acc = _online_update(x, slope_col * page_pos.astype(F32), vp, m, l, acc)
    m_scr[...] = m
    l_scr[...] = l
    acc_scr[...] = acc

    @pl.when(j == nj - 1)
    def _():
        kn = kn_ref[...].reshape(t_new * heads, LANES).astype(BF16)
        vn = vn_ref[...].reshape(t_new * heads, LANES).astype(BF16)
        x = _dot_nt(w, kn) + bias(t_new * heads, True)
        _, l2, acc2 = _online_update(x, jnp.zeros((rows, 1), F32), vn, m, l, acc)
        o = acc2 / l2
        lam = _diff_lambda(lq1_ref, lk1_ref, lq2_ref, lk2_ref, lam_init)
        for h in range(heads):
            r0 = h * 2 * t_new
            oh = o[r0:r0 + t_new] - lam * o[r0 + t_new:r0 + 2 * t_new]
            gate = g_ref[:, h * LANES:(h + 1) * LANES]
            o_ref[:, h * LANES:(h + 1) * LANES] = _subln_gate(oh, sg_ref, gate, lam_init)


def _attn_decode(page_table, q, kn, vn, ug, sg, lq1, lk1, lq2, lk2, cache_k, cache_v, *,
                 layer, lam_init, pages_per_step=4):
    batch, n_pages = page_table.shape
    _, _, page, heads, _ = cache_k.shape
    t_new = q.shape[0] // batch
    width = q.shape[1]
    gp = pages_per_step
    rows = heads * 2 * t_new
    cols = page * heads
    small = lambda b, j, pt: (0, 0)

    def page_spec(gi):
        return pl.BlockSpec((None, None, page, heads, LANES),
                            lambda b, j, pt: (layer, pt[b, j * gp + gi], 0, 0, 0))

    in_specs = [
        pl.BlockSpec((t_new, width), lambda b, j, pt: (b, 0)),
        pl.BlockSpec((t_new, heads, LANES), lambda b, j, pt: (b, 0, 0)),
        pl.BlockSpec((t_new, heads, LANES), lambda b, j, pt: (b, 0, 0)),
        pl.BlockSpec((t_new, width), lambda b, j, pt: (b, 2)),
        pl.BlockSpec(sg.shape, small),
        pl.BlockSpec(lq1.shape, small), pl.BlockSpec(lk1.shape, small),
        pl.BlockSpec(lq2.shape, small), pl.BlockSpec(lk2.shape, small),
    ] + [page_spec(gi) for gi in range(gp)] * 2
    grid_spec = pltpu.PrefetchScalarGridSpec(
        num_scalar_prefetch=1,
        grid=(batch, n_pages // gp),
        in_specs=in_specs,
        out_specs=pl.BlockSpec((t_new, width), lambda b, j, pt: (b, 0)),
        scratch_shapes=[pltpu.VMEM((rows, LANES), BF16), pltpu.VMEM((rows, cols), F32),
                        pltpu.VMEM((rows, 1), F32), pltpu.VMEM((rows, 1), F32),
                        pltpu.VMEM((rows, DA_VDIM), F32)],
    )
    page_bytes = page * heads * LANES * 4
    return pl.pallas_call(
        functools.partial(_decode_body, pages_per_step=gp, past_len=n_pages * page, lam_init=lam_init),
        grid_spec=grid_spec,
        out_shape=jax.ShapeDtypeStruct(q.shape, F32),
        compiler_params=pltpu.CompilerParams(
            dimension_semantics=("arbitrary", "arbitrary"),
            vmem_limit_bytes=_vmem_limit(4 * gp * page_bytes + 8 * rows * cols * 4)),
        name="attn_decode",
    )(page_table, q, kn, vn, ug, sg, lq1, lk1, lq2, lk2, *([cache_k] * gp), *([cache_v] * gp))


def _out_body(mrg_ref, mda_ref, w_ref, x_ref, g_ref, b_ref, of_ref, ob_ref, *, alpha):
    half = mrg_ref.shape[1]
    y = jnp.dot(mrg_ref[...].astype(BF16), w_ref[:half, :], preferred_element_type=F32)
    y = y + jnp.dot(mda_ref[...].astype(BF16), w_ref[half:, :], preferred_element_type=F32)
    z = alpha * x_ref[...] + y
    mu = jnp.mean(z, axis=-1, keepdims=True)
    zc = z - mu
    var = jnp.mean(zc * zc, axis=-1, keepdims=True)
    out = zc * lax.rsqrt(var + LN_EPS) * g_ref[...] + b_ref[...]
    of_ref[...] = out
    ob_ref[...] = out.astype(BF16)


def _out_proj(mrg, mda, w, x, g, b, *, alpha, bm):
    n, d = x.shape
    half = mrg.shape[1]
    rowb = lambda i: (i, 0)
    const = lambda i: (0, 0)
    nbytes = 2 * (2 * bm * half * 4 + w.size * 2 + bm * d * 4 + bm * d * 6)
    return pl.pallas_call(
        functools.partial(_out_body, alpha=alpha),
        grid=(n // bm,),
        in_specs=[pl.BlockSpec((bm, half), rowb), pl.BlockSpec((bm, half), rowb),
                  pl.BlockSpec(w.shape, const), pl.BlockSpec((bm, d), rowb),
                  pl.BlockSpec(g.shape, const), pl.BlockSpec(b.shape, const)],
        out_specs=[pl.BlockSpec((bm, d), rowb), pl.BlockSpec((bm, d), rowb)],
        out_shape=[jax.ShapeDtypeStruct((n, d), F32), jax.ShapeDtypeStruct((n, d), BF16)],
        compiler_params=pltpu.CompilerParams(
            dimension_semantics=("arbitrary",),
            vmem_limit_bytes=_vmem_limit(nbytes + 4 * bm * d * 4)),
        name="out_proj",
    )(mrg, mda, w, x, g, b)


def _gate_weights(w_a, w_x):
    per = GATE_TILE // w_a.shape[1]

    def tiles(w):
        w4 = w.reshape(-1, per, w.shape[1], w.shape[2])
        eye = jnp.eye(per, dtype=w.dtype)
        return jnp.einsum("gncd,nm->gncmd", w4, eye).reshape(-1, GATE_TILE, GATE_TILE)

    return jnp.concatenate([tiles(w_a), tiles(w_x)], axis=2).astype(BF16)


def kernel(x_prompt, x_sample, cache_k, cache_v, state_h, state_conv, page_table, w_in, conv_w, conv_b,
           w_ga, b_ga, w_gx, b_gx, lru_lam, lq1, lk1, lq2, lk2, subln_g, w_out, ln_g, ln_b):
    depth = w_in.shape[0]
    bp, seq, d_model = x_prompt.shape
    bs, t_new, _ = x_sample.shape
    heads = cache_k.shape[3]
    d_rg = conv_w.shape[2]
    alpha = (2 * depth) ** 0.25
    slopes = jnp.exp2(-8.0 * jnp.arange(1, heads + 1, dtype=F32) / heads)

    xp = x_prompt.reshape(bp * seq, d_model)
    xs = x_sample.reshape(bs * t_new, d_model)
    xp_b, xs_b = xp.astype(BF16), xs.astype(BF16)
    zero_h = jnp.zeros((bp, 1, d_rg), F32)
    zero_c = jnp.zeros((bp, CONV_W - 1, d_rg), F32)

    kbig = vbig = None
    ks_l, vs_l, hp_l, cp_l, hs_l, cs_l = [], [], [], [], [], []
    for l in range(depth):
        lam_init = _lambda_init(l)
        w_l = w_in[l].astype(BF16)
        ug_p, ug_s = _proj(xp_b, xs_b, w_l, col_blocks=(0, 1, 5), mode="f32")
        q_p, q_s = _proj(xp_b, xs_b, w_l, col_blocks=(2,), mode="bf16")
        k_p, kbig, k_s = _proj(xp_b, xs_b, w_l, col_blocks=(3,), mode="kv", big=kbig, layer=l, depth=depth)
        v_p, vbig, v_s = _proj(xp_b, xs_b, w_l, col_blocks=(4,), mode="kv", big=vbig, layer=l, depth=depth)

        wg = _gate_weights(w_ga[l], w_gx[l])
        row = lambda a: a.reshape(1, -1)
        rg_args = (conv_w[l], row(conv_b[l]), wg, row(b_ga[l]), row(b_gx[l]), row(lru_lam[l]))
        mrg_p, ht_p, ct_p = _rglru(ug_p, *rg_args, zero_h, zero_c,
                                   batch=bp, seq=seq, bt=256, out_dtype=BF16)
        mrg_s, ht_s, ct_s = _rglru(ug_s, *rg_args, state_h[l].reshape(bs, 1, d_rg), state_conv[l],
                                   batch=bs, seq=t_new, bt=t_new, out_dtype=F32)

        da_args = (row(subln_g[l]), row(lq1[l]), row(lk1[l]), row(lq2[l]), row(lk2[l]))
        mda_p = _attn_prompt(slopes, q_p, k_p, v_p, ug_p, *da_args,
                             batch=bp, seq=seq, heads=heads, lam_init=lam_init)
        k_s3 = k_s.reshape(bs * t_new, heads, LANES)
        v_s3 = v_s.reshape(bs * t_new, heads, LANES)
        mda_s = _attn_decode(page_table, q_s, k_s3, v_s3, ug_s, *da_args, cache_k, cache_v,
                             layer=l, lam_init=lam_init)

        w_o = w_out[l].astype(BF16)
        xp, xp_b = _out_proj(mrg_p, mda_p, w_o, xp, row(ln_g[l]), row(ln_b[l]), alpha=alpha, bm=512)
        xs, xs_b = _out_proj(mrg_s, mda_s, w_o, xs, row(ln_g[l]), row(ln_b[l]), alpha=alpha, bm=bs * t_new)

        ks_l.append(k_s3.reshape(bs, t_new, heads, LANES))
        vs_l.append(v_s3.reshape(bs, t_new, heads, LANES))
        hp_l.append(ht_p.reshape(bp, d_rg)); cp_l.append(ct_p)
        hs_l.append(ht_s.reshape(bs, d_rg)); cs_l.append(ct_s)

    return (xp.reshape(bp, seq, d_model), xs.reshape(bs, t_new, d_model),
            kbig.reshape(depth, bp, seq, heads, LANES), vbig.reshape(depth, bp, seq, heads, LANES),
            jnp.stack(hp_l), jnp.stack(cp_l),
            jnp.stack(ks_l), jnp.stack(vs_l), jnp.stack(hs_l), jnp.stack(cs_l))
```

```python
import functools
import math

import jax
import jax.numpy as jnp
from jax import lax
from jax.experimental import pallas as pl
from jax.experimental.pallas import tpu as pltpu

F32 = jnp.float32
BF16 = jnp.bfloat16

H_RG = 16
CONV_W = 4
LRU_C = 8.0
DA_HEAD = 64
DA_VDIM = 2 * DA_HEAD
LN_EPS = 1e-5
SUBLN_EPS = 1e-5
MASKED = -1e30
LOG2_E = math.log2(math.e)
Q_SCALE_LOG2 = DA_HEAD ** -0.5 * LOG2_E
ONES_ROWS = 16

LANES = 128
SUBLANES = 8
MXU_TILE = 256
GATE_TILE = MXU_TILE
V7X_VMEM_BYTES = 64 * 1024 * 1024


def _vmem_limit(nbytes):
    return int(min(V7X_VMEM_BYTES - (8 << 20), nbytes + (16 << 20)))


def _lambda_init(layer):
    return 0.8 - 0.6 * math.exp(-0.3 * layer)


def _diff_lambda(lq1_ref, lk1_ref, lq2_ref, lk2_ref, lam_init):
    a = jnp.sum(lq1_ref[...] * lk1_ref[...], axis=1, keepdims=True)
    b = jnp.sum(lq2_ref[...] * lk2_ref[...], axis=1, keepdims=True)
    return jnp.exp(a) - jnp.exp(b) + lam_init


def _silu(x):
    return x * jax.nn.sigmoid(x)


def _fold_rows(x, op, chains=4):
    slabs = [x[i:i + SUBLANES] for i in range(0, x.shape[0], SUBLANES)]
    accs = slabs[:chains]
    for i, s in enumerate(slabs[chains:]):
        accs[i % chains] = op(accs[i % chains], s)
    while len(accs) > 1:
        accs = [op(accs[i], accs[i + 1]) for i in range(0, len(accs), 2)]
    return accs[0]


def _dot_nt(a, b):
    return lax.dot_general(a, b, (((1,), (1,)), ((), ())), preferred_element_type=F32)


def _proj_body(*refs, mode, aliased):
    xp_ref, xs_ref, w_ref = refs[:3]
    outs = refs[3 + (1 if aliased else 0):]
    w = w_ref[...]
    acc = jnp.dot(xp_ref[...], w, preferred_element_type=F32)
    if mode == "f32":
        op_ref, os_ref = outs
        op_ref[...] = acc
    elif mode == "q":
        op_ref, os_ref = outs
        op_ref[...] = (acc * Q_SCALE_LOG2).astype(BF16)
    else:
        op_ref, o3_ref, os_ref = outs
        op_ref[...] = acc.astype(BF16)
        for h in range(o3_ref.shape[1]):
            o3_ref[:, h, :] = acc[:, h * LANES:(h + 1) * LANES]

    @pl.when(pl.program_id(1) == 0)
    def _():
        os_ref[...] = jnp.dot(xs_ref[...], w, preferred_element_type=F32)


def _proj(xp, xs, w, *, col_blocks, mode, big=None, layer=0, depth=1, bm=1024, bn=1024):
    n_p, d = xp.shape
    n_s = xs.shape[0]
    ncol = len(col_blocks)
    if col_blocks == (0, 1, 5):
        wcol = lambda j: j + 3 * (j // 2)
    else:
        assert ncol == 1
        c0 = col_blocks[0]
        wcol = lambda j: c0
    in_specs = [
        pl.BlockSpec((bm, d), lambda j, i: (i, 0)),
        pl.BlockSpec((n_s, d), lambda j, i: (0, 0)),
        pl.BlockSpec((d, bn), lambda j, i: (0, wcol(j))),
    ]
    args = [xp, xs, w]
    aliases = {}
    nbytes = 2 * (bm * d * 2 + n_s * d * 2 + d * bn * 2 + n_s * bn * 4)
    if mode == "kv":
        heads = bn // LANES
        out_shape = [jax.ShapeDtypeStruct((n_p, bn), BF16),
                     jax.ShapeDtypeStruct((depth, n_p, heads, LANES), F32),
                     jax.ShapeDtypeStruct((n_s, bn), F32)]
        out_specs = [pl.BlockSpec((bm, bn), lambda j, i: (i, 0)),
                     pl.BlockSpec((None, bm, heads, LANES), lambda j, i: (layer, i, 0, 0)),
                     pl.BlockSpec((n_s, bn), lambda j, i: (0, 0))]
        nbytes += 2 * (bm * bn * 2 + bm * bn * 4)
        if big is not None:
            in_specs.append(pl.BlockSpec(memory_space=pl.ANY))
            args.append(big)
            aliases = {3: 1}
    else:
        odt = F32 if mode == "f32" else BF16
        out_shape = [jax.ShapeDtypeStruct((n_p, ncol * bn), odt),
                     jax.ShapeDtypeStruct((n_s, ncol * bn), F32)]
        out_specs = [pl.BlockSpec((bm, bn), lambda j, i: (i, j)),
                     pl.BlockSpec((n_s, bn), lambda j, i: (0, j))]
        nbytes += 2 * bm * bn * 4
    return pl.pallas_call(
        functools.partial(_proj_body, mode=mode, aliased=big is not None),
        grid=(ncol, n_p // bm),
        in_specs=in_specs,
        out_specs=out_specs,
        out_shape=out_shape,
        input_output_aliases=aliases,
        compiler_params=pltpu.CompilerParams(
            dimension_semantics=("arbitrary", "arbitrary"),
            vmem_limit_bytes=_vmem_limit(nbytes + bm * bn * 4)),
        name="proj_" + mode,
    )(*args)


def _rglru_body(u_ref, g_ref, cw_ref, cb_ref, wg_ref, ba_ref, bx_ref, lam_ref, h0_ref, c0_ref,
                o_ref, ht_ref, ct_ref, ext_scr, a_scr, b_scr, hs_scr, car_scr):
    bt, c = u_ref.shape
    t_blk = pl.program_id(1)
    halo = CONV_W - 1

    @pl.when(t_blk == 0)
    def _():
        ext_scr[SUBLANES - halo:SUBLANES, :] = c0_ref[...]
        car_scr[...] = jnp.broadcast_to(h0_ref[...], car_scr.shape)

    u = u_ref[...]
    ext_scr[SUBLANES:SUBLANES + bt, :] = u
    uc = cb_ref[...] + cw_ref[halo:halo + 1, :] * u
    for j in range(halo):
        lag = halo - j
        uc = uc + cw_ref[j:j + 1, :] * ext_scr[SUBLANES - lag:SUBLANES - lag + bt, :]
    tail = ext_scr[SUBLANES + bt - halo:SUBLANES + bt, :]
    ct_ref[...] = tail
    ext_scr[SUBLANES - halo:SUBLANES, :] = tail

    ucb = uc.astype(BF16)
    ngrp = c // GATE_TILE
    pre_a, pre_x = [], []
    for gi in range(ngrp):
        pre = jnp.dot(ucb[:, gi * GATE_TILE:(gi + 1) * GATE_TILE], wg_ref[gi],
                      preferred_element_type=F32)
        pre_a.append(pre[:, :GATE_TILE])
        pre_x.append(pre[:, GATE_TILE:])
    r = jax.nn.sigmoid(jnp.concatenate(pre_a, axis=1) + ba_ref[...])
    gate_i = jax.nn.sigmoid(jnp.concatenate(pre_x, axis=1) + bx_ref[...])
    z = -lam_ref[...]
    softplus = jnp.maximum(z, 0.0) + jnp.log1p(jnp.exp(-jnp.abs(z)))
    log_a = (-LRU_C * r) * softplus
    a = jnp.exp(log_a)
    one_minus_a2 = -jnp.tanh(log_a) * (a * a + 1.0)
    a_scr[...] = a
    b_scr[...] = jnp.sqrt(one_minus_a2) * (gate_i * uc)

    row = lax.broadcasted_iota(jnp.int32, (SUBLANES, c), 0)

    def scan_tile(k, carry):
        sl = pl.ds(pl.multiple_of(k * SUBLANES, SUBLANES), SUBLANES)
        av = a_scr[sl, :]
        bv = b_scr[sl, :]
        for s in (1, 2, 4):
            a_sh = jnp.where(row >= s, pltpu.roll(av, s, 0), 1.0)
            b_sh = jnp.where(row >= s, pltpu.roll(bv, s, 0), 0.0)
            bv = bv + av * b_sh
            av = av * a_sh
        h = bv + av * carry
        hs_scr[sl, :] = h
        return jnp.broadcast_to(h[SUBLANES - 1:SUBLANES, :], h.shape)

    carry = lax.fori_loop(0, bt // SUBLANES, scan_tile, car_scr[...])
    car_scr[...] = carry
    ht_ref[...] = carry[0:1, :]
    o_ref[...] = (hs_scr[...] * _silu(g_ref[...])).astype(o_ref.dtype)


def _rglru(ug, cw, cb, wg, ba, bx, lam, h0, c0, *, batch, seq, bt, out_dtype):
    c = cw.shape[1]
    nt = seq // bt
    halo = CONV_W - 1
    row_blk = lambda b, t: (b * nt + t, 0)
    const2 = lambda b, t: (0, 0)
    in_specs = [
        pl.BlockSpec((bt, c), row_blk),
        pl.BlockSpec((bt, c), lambda b, t: (b * nt + t, 1)),
        pl.BlockSpec(cw.shape, const2),
        pl.BlockSpec(cb.shape, const2),
        pl.BlockSpec(wg.shape, lambda b, t: (0, 0, 0)),
        pl.BlockSpec(ba.shape, const2),
        pl.BlockSpec(bx.shape, const2),
        pl.BlockSpec(lam.shape, const2),
        pl.BlockSpec((None, 1, c), lambda b, t: (b, 0, 0)),
        pl.BlockSpec((None, halo, c), lambda b, t: (b, 0, 0)),
    ]
    out_shape = [jax.ShapeDtypeStruct((batch * seq, c), out_dtype),
                 jax.ShapeDtypeStruct((batch, 1, c), F32),
                 jax.ShapeDtypeStruct((batch, halo, c), F32)]
    out_specs = [pl.BlockSpec((bt, c), row_blk),
                 pl.BlockSpec((None, 1, c), lambda b, t: (b, 0, 0)),
                 pl.BlockSpec((None, halo, c), lambda b, t: (b, 0, 0))]
    scratch = [pltpu.VMEM((bt + SUBLANES, c), F32), pltpu.VMEM((bt, c), F32),
               pltpu.VMEM((bt, c), F32), pltpu.VMEM((bt, c), F32),
               pltpu.VMEM((SUBLANES, c), F32)]
    nbytes = 2 * (2 * bt * c * 4 + bt * c * 4) + 4 * bt * c * 4 + wg.size * 2 * 2
    return pl.pallas_call(
        _rglru_body,
        grid=(batch, nt),
        in_specs=in_specs,
        out_specs=out_specs,
        out_shape=out_shape,
        scratch_shapes=scratch,
        compiler_params=pltpu.CompilerParams(
            dimension_semantics=("arbitrary", "arbitrary"),
            vmem_limit_bytes=_vmem_limit(nbytes + 8 * bt * c * 4)),
        name="rglru",
    )(ug, ug, cw, cb, wg, ba, bx, lam, h0, c0)


def _subln_gate(o, sg_ref, gate, lam_init):
    ms = jnp.mean(o * o, axis=-1, keepdims=True)
    o = o * lax.rsqrt(ms + SUBLN_EPS) * sg_ref[...]
    return (o * (1.0 - lam_init)) * _silu(gate)


def _attn_body(slope_ref, q_ref, k_ref, v_ref, g_ref, sg_ref, lq1_ref, lk1_ref, lq2_ref, lk2_ref,
               o_ref, vt_scr, kb_scr, mask_scr, x_scr, xm_scr, m_scr, acc_scr, *, lam_init):
    blk = q_ref.shape[0]
    nk = vt_scr.shape[0]
    dv = v_ref.shape[1]
    b = pl.program_id(0)
    h = pl.program_id(1)
    qi = pl.program_id(2)
    slope = slope_ref[h] * LOG2_E

    @pl.when(jnp.logical_and(jnp.logical_and(b == 0, h == 0), qi == 0))
    def _():
        key = lax.broadcasted_iota(jnp.int32, (blk, blk), 0)
        qry = lax.broadcasted_iota(jnp.int32, (blk, blk), 1)
        mask_scr[...] = jnp.where(key <= qry, 0.0, MASKED)

    @pl.when(qi == 0)
    def _():
        ones = jnp.ones((ONES_ROWS, blk), BF16)
        for j in range(nk):
            vt_scr[j, :dv, :] = v_ref[j * blk:(j + 1) * blk, :].astype(F32).T.astype(BF16)
            vt_scr[j, dv:, :] = ones
        key = lax.broadcasted_iota(jnp.int32, (blk, LANES), 0)
        kb_scr[...] = slope * key.astype(F32)

    q = q_ref[...]
    lane = lax.broadcasted_iota(jnp.int32, q.shape, 1)
    zero = jnp.zeros_like(q)
    q_maps = (jnp.where(lane < DA_HEAD, q, zero), jnp.where(lane >= DA_HEAD, q, zero))
    acc_scr[...] = jnp.zeros(acc_scr.shape, F32)
    m_scr[...] = jnp.full(m_scr.shape, MASKED, F32)

    def scores(ki, diagonal):
        koff = pl.multiple_of(ki * blk, blk)
        k = k_ref[pl.ds(koff, blk), :]
        kb = kb_scr[...] + slope * ((ki - qi) * blk).astype(F32)
        kb = pltpu.repeat(kb, blk // LANES, axis=1)
        if diagonal:
            kb = kb + mask_scr[...]
        for c in range(2):
            x = _dot_nt(k, q_maps[c]) + kb
            x_scr[c] = x
            xm_scr[c] = jnp.max(_fold_rows(x, jnp.maximum), axis=0, keepdims=True)

    def softmax_pv(ki):
        vt = vt_scr[ki]
        for c in range(2):
            m = m_scr[c]
            m_new = jnp.maximum(m, xm_scr[c])
            alpha = jnp.exp2(m - m_new)
            p = jnp.exp2(x_scr[c] - m_new)
            acc_scr[c] = alpha * acc_scr[c] + jnp.dot(vt, p.astype(BF16), preferred_element_type=F32)
            m_scr[c] = m_new

    def block(ki, diagonal):
        scores(ki, diagonal)

        @pl.when(slope > 0.0)
        def _():
            softmax_pv(ki)

    def off_diagonal(ki, carry):
        block(ki, False)
        return carry

    lax.fori_loop(0, qi, off_diagonal, 0)
    block(qi, True)

    lam = _diff_lambda(lq1_ref, lk1_ref, lq2_ref, lk2_ref, lam_init)
    ot = (acc_scr[0, :dv, :] / acc_scr[0, dv:dv + 1, :]
          - lam * (acc_scr[1, :dv, :] / acc_scr[1, dv:dv + 1, :]))
    ms = jnp.mean(ot * ot, axis=0, keepdims=True)
    o = (ot * lax.rsqrt(ms + SUBLN_EPS)).T
    o = (o * sg_ref[...]) * (1.0 - lam_init)
    o_ref[...] = (o * _silu(g_ref[...])).astype(o_ref.dtype)


def _attn_prompt(slopes, q, k, v, ug, sg, lq1, lk1, lq2, lk2, *, batch, seq, heads, lam_init, blk=512):
    nq = seq // blk
    gate_col0 = 2 * (q.shape[1] // LANES)
    qmap = lambda b, h, qi: (b * nq + qi, h)
    kmap = lambda b, h, qi: (b, h)
    small = lambda b, h, qi: (0, 0)
    in_specs = [
        pl.BlockSpec(memory_space=pltpu.SMEM),
        pl.BlockSpec((blk, LANES), qmap),
        pl.BlockSpec((seq, LANES), kmap),
        pl.BlockSpec((seq, LANES), kmap),
        pl.BlockSpec((blk, LANES), lambda b, h, qi: (b * nq + qi, gate_col0 + h)),
        pl.BlockSpec(sg.shape, small),
        pl.BlockSpec(lq1.shape, small), pl.BlockSpec(lk1.shape, small),
        pl.BlockSpec(lq2.shape, small), pl.BlockSpec(lk2.shape, small),
    ]
    return pl.pallas_call(
        functools.partial(_attn_body, lam_init=lam_init),
        grid=(batch, heads, nq),
        in_specs=in_specs,
        out_specs=pl.BlockSpec((blk, LANES), qmap),
        out_shape=jax.ShapeDtypeStruct(q.shape, BF16),
        scratch_shapes=[pltpu.VMEM((nq, DA_VDIM + ONES_ROWS, blk), BF16), pltpu.VMEM((blk, LANES), F32),
                        pltpu.VMEM((blk, blk), F32), pltpu.VMEM((2, blk, blk), F32),
                        pltpu.VMEM((2, 1, blk), F32), pltpu.VMEM((2, 1, blk), F32),
                        pltpu.VMEM((2, DA_VDIM + ONES_ROWS, blk), F32)],
        compiler_params=pltpu.CompilerParams(
            dimension_semantics=("arbitrary", "arbitrary", "arbitrary"),
            vmem_limit_bytes=_vmem_limit(16 * blk * blk * 4)),
        name="attn_prompt",
    )(slopes, q, k, v, ug, sg, lq1, lk1, lq2, lk2)


def _fold_lanes(x, op):
    tiles = [x[:, i:i + LANES] for i in range(0, x.shape[1], LANES)]
    while len(tiles) > 1:
        tiles = [op(tiles[i], tiles[i + 1]) for i in range(0, len(tiles), 2)]
    return tiles[0]


def _decode_body(pt_ref, q_ref, kn_ref, vn_ref, g_ref, sg_ref, lq1_ref, lk1_ref, lq2_ref, lk2_ref,
                 *rest, pages_per_step, past_len, lam_init):
    del pt_ref
    gp = pages_per_step
    k_refs = rest[:gp]
    v_refs = rest[gp:2 * gp]
    o_ref = rest[2 * gp]
    w_scr, c_scr, x_scr, m_scr, l_scr, acc_scr = rest[2 * gp + 1:]
    page, heads, dv = v_refs[0].shape
    t_new = q_ref.shape[0]
    rows = heads * 2 * t_new
    cols = page * heads
    b = pl.program_id(0)
    j = pl.program_id(1)
    nj = pl.num_programs(1)

    def row_slope(shape):
        hq = lax.broadcasted_iota(jnp.int32, shape, 0) // (2 * t_new)
        return jnp.exp2(-(hq + 1).astype(F32)) * LOG2_E

    def bias(ncols, causal):
        r = lax.broadcasted_iota(jnp.int32, (rows, ncols), 0)
        c = lax.broadcasted_iota(jnp.int32, (rows, ncols), 1)
        hq, t = r // (2 * t_new), r % t_new
        s, hk = c // heads, c % heads
        keep = hq == hk
        if causal:
            keep = jnp.logical_and(keep, s <= t)
        return jnp.where(keep, row_slope((rows, ncols)) * (s - t).astype(F32), MASKED)

    @pl.when(jnp.logical_and(b == 0, j == 0))
    def _():
        c_scr[...] = bias(cols, False)

    @pl.when(j == 0)
    def _():
        q = q_ref[...]
        lane = lax.broadcasted_iota(jnp.int32, (t_new, LANES), 1)
        parts = []
        for h in range(heads):
            qh = q[:, h * LANES:(h + 1) * LANES] * Q_SCALE_LOG2
            parts.append(jnp.where(lane < DA_HEAD, qh, 0.0))
            parts.append(jnp.where(lane >= DA_HEAD, qh, 0.0))
        w_scr[...] = jnp.concatenate(parts, axis=0).astype(BF16)
        m_scr[...] = jnp.full(m_scr.shape, MASKED, F32)
        l_scr[...] = jnp.zeros(l_scr.shape, F32)
        acc_scr[...] = jnp.zeros(acc_scr.shape, F32)

    w = w_scr[...]
    slope_col = row_slope((rows, 1))
    offs = [slope_col * ((j * gp + gi) * page - past_len).astype(F32) for gi in range(gp)]

    top = None
    for gi in range(gp):
        page_top = None
        for c0 in range(0, cols, MXU_TILE):
            kp = k_refs[gi][c0 // heads:(c0 + MXU_TILE) // heads].reshape(MXU_TILE, LANES).astype(BF16)
            x = _dot_nt(w, kp) + c_scr[:, c0:c0 + MXU_TILE]
            x_scr[gi, :, c0:c0 + MXU_TILE] = x
            tile_top = _fold_lanes(x, jnp.maximum)
            page_top = tile_top if page_top is None else jnp.maximum(page_top, tile_top)
        page_top = page_top + offs[gi]
        top = page_top if top is None else jnp.maximum(top, page_top)
    m_old = m_scr[...]
    m_new = jnp.maximum(m_old, jnp.max(top, axis=1, keepdims=True))
    m_scr[...] = m_new

    pv = None
    psum = None
    for gi in range(gp):
        p = jnp.exp2(x_scr[gi] - (m_new - offs[gi]))
        vp = v_refs[gi][...].reshape(cols, LANES).astype(BF16)
        part = jnp.dot(p.astype(BF16), vp, preferred_element_type=F32)
        part_sum = _fold_lanes(p, jnp.add)
        pv = part if pv is None else pv + part
        psum = part_sum if psum is None else psum + part_sum
    alpha = jnp.exp2(m_old - m_new)
    acc = alpha * acc_scr[...] + pv
    l = alpha * l_scr[...] + jnp.sum(psum, axis=1, keepdims=True)
    acc_scr[...] = acc
    l_scr[...] = l

    @pl.when(j == nj - 1)
    def _():
        kn = kn_ref[...].reshape(t_new * heads, LANES).astype(BF16)
        vn = vn_ref[...].reshape(t_new * heads, LANES).astype(BF16)
        x = _dot_nt(w, kn) + bias(t_new * heads, True)
        m_fin = jnp.maximum(m_new, jnp.max(x, axis=1, keepdims=True))
        p = jnp.exp2(x - m_fin)
        beta = jnp.exp2(m_new - m_fin)
        acc2 = beta * acc + jnp.dot(p.astype(BF16), vn, preferred_element_type=F32)
        o = acc2 / (beta * l + jnp.sum(p, axis=1, keepdims=True))
        lam = _diff_lambda(lq1_ref, lk1_ref, lq2_ref, lk2_ref, lam_init)
        for h in range(heads):
            r0 = h * 2 * t_new
            oh = o[r0:r0 + t_new] - lam * o[r0 + t_new:r0 + 2 * t_new]
            gate = g_ref[:, h * LANES:(h + 1) * LANES]
            o_ref[:, h * LANES:(h + 1) * LANES] = _subln_gate(oh, sg_ref, gate, lam_init)


def _attn_decode(page_table, q, kn, vn, ug, sg, lq1, lk1, lq2, lk2, cache_k, cache_v, *,
                 layer, lam_init, pages_per_step=8):
    batch, n_pages = page_table.shape
    _, _, page, heads, _ = cache_k.shape
    t_new = q.shape[0] // batch
    width = q.shape[1]
    gp = pages_per_step
    rows = heads * 2 * t_new
    cols = page * heads
    small = lambda b, j, pt: (0, 0)

    def page_spec(gi):
        return pl.BlockSpec((None, None, page, heads, LANES),
                            lambda b, j, pt: (layer, pt[b, j * gp + gi], 0, 0, 0))

    in_specs = [
        pl.BlockSpec((t_new, width), lambda b, j, pt: (b, 0)),
        pl.BlockSpec((t_new, heads, LANES), lambda b, j, pt: (b, 0, 0)),
        pl.BlockSpec((t_new, heads, LANES), lambda b, j, pt: (b, 0, 0)),
        pl.BlockSpec((t_new, width), lambda b, j, pt: (b, 2)),
        pl.BlockSpec(sg.shape, small),
        pl.BlockSpec(lq1.shape, small), pl.BlockSpec(lk1.shape, small),
        pl.BlockSpec(lq2.shape, small), pl.BlockSpec(lk2.shape, small),
    ] + [page_spec(gi) for gi in range(gp)] * 2
    grid_spec = pltpu.PrefetchScalarGridSpec(
        num_scalar_prefetch=1,
        grid=(batch, n_pages // gp),
        in_specs=in_specs,
        out_specs=pl.BlockSpec((t_new, width), lambda b, j, pt: (b, 0)),
        scratch_shapes=[pltpu.VMEM((rows, LANES), BF16), pltpu.VMEM((rows, cols), F32),
                        pltpu.VMEM((gp, rows, cols), F32), pltpu.VMEM((rows, 1), F32),
                        pltpu.VMEM((rows, 1), F32), pltpu.VMEM((rows, DA_VDIM), F32)],
    )
    page_bytes = page * heads * LANES * 4
    return pl.pallas_call(
        functools.partial(_decode_body, pages_per_step=gp, past_len=n_pages * page, lam_init=lam_init),
        grid_spec=grid_spec,
        out_shape=jax.ShapeDtypeStruct(q.shape, F32),
        compiler_params=pltpu.CompilerParams(
            dimension_semantics=("arbitrary", "arbitrary"),
            vmem_limit_bytes=_vmem_limit(4 * gp * page_bytes + (gp + 4) * rows * cols * 4)),
        name="attn_decode",
    )(page_table, q, kn, vn, ug, sg, lq1, lk1, lq2, lk2, *([cache_k] * gp), *([cache_v] * gp))


def _out_body(mrg_ref, mda_ref, w_ref, x_ref, g_ref, b_ref, of_ref, ob_ref, *, alpha):
    half = mrg_ref.shape[1]
    y = jnp.dot(mrg_ref[...].astype(BF16), w_ref[:half, :], preferred_element_type=F32)
    y = y + jnp.dot(mda_ref[...].astype(BF16), w_ref[half:, :], preferred_element_type=F32)
    z = alpha * x_ref[...] + y
    mu = jnp.mean(z, axis=-1, keepdims=True)
    zc = z - mu
    var = jnp.mean(zc * zc, axis=-1, keepdims=True)
    out = zc * lax.rsqrt(var + LN_EPS) * g_ref[...] + b_ref[...]
    of_ref[...] = out
    ob_ref[...] = out.astype(BF16)


def _out_proj(mrg, mda, w, x, g, b, *, alpha, bm):
    n, d = x.shape
    half = mrg.shape[1]
    rowb = lambda i: (i, 0)
    const = lambda i: (0, 0)
    nbytes = 2 * (2 * bm * half * 4 + w.size * 2 + bm * d * 4 + bm * d * 6)
    return pl.pallas_call(
        functools.partial(_out_body, alpha=alpha),
        grid=(n // bm,),
        in_specs=[pl.BlockSpec((bm, half), rowb), pl.BlockSpec((bm, half), rowb),
                  pl.BlockSpec(w.shape, const), pl.BlockSpec((bm, d), rowb),
                  pl.BlockSpec(g.shape, const), pl.BlockSpec(b.shape, const)],
        out_specs=[pl.BlockSpec((bm, d), rowb), pl.BlockSpec((bm, d), rowb)],
        out_shape=[jax.ShapeDtypeStruct((n, d), F32), jax.ShapeDtypeStruct((n, d), BF16)],
        compiler_params=pltpu.CompilerParams(
            dimension_semantics=("arbitrary",),
            vmem_limit_bytes=_vmem_limit(nbytes + 4 * bm * d * 4)),
        name="out_proj",
    )(mrg, mda, w, x, g, b)


def _gate_weights(w_a, w_x):
    per = GATE_TILE // w_a.shape[1]

    def tiles(w):
        w4 = w.reshape(-1, per, w.shape[1], w.shape[2])
        eye = jnp.eye(per, dtype=w.dtype)
        return jnp.einsum("gncd,nm->gncmd", w4, eye).reshape(-1, GATE_TILE, GATE_TILE)

    return jnp.concatenate([tiles(w_a), tiles(w_x)], axis=2).astype(BF16)


def kernel(x_prompt, x_sample, cache_k, cache_v, state_h, state_conv, page_table, w_in, conv_w, conv_b,
           w_ga, b_ga, w_gx, b_gx, lru_lam, lq1, lk1, lq2, lk2, subln_g, w_out, ln_g, ln_b):
    depth = w_in.shape[0]
    bp, seq, d_model = x_prompt.shape
    bs, t_new, _ = x_sample.shape
    heads = cache_k.shape[3]
    d_rg = conv_w.shape[2]
    alpha = (2 * depth) ** 0.25
    slopes = jnp.exp2(-8.0 * jnp.arange(1, heads + 1, dtype=F32) / heads)

    xp = x_prompt.reshape(bp * seq, d_model)
    xs = x_sample.reshape(bs * t_new, d_model)
    xp_b, xs_b = xp.astype(BF16), xs.astype(BF16)
    zero_h = jnp.zeros((bp, 1, d_rg), F32)
    zero_c = jnp.zeros((bp, CONV_W - 1, d_rg), F32)

    kbig = vbig = None
    ks_l, vs_l, hp_l, cp_l, hs_l, cs_l = [], [], [], [], [], []
    for l in range(depth):
        lam_init = _lambda_init(l)
        w_l = w_in[l].astype(BF16)
        ug_p, ug_s = _proj(xp_b, xs_b, w_l, col_blocks=(0, 1, 5), mode="f32")
        q_p, q_s = _proj(xp_b, xs_b, w_l, col_blocks=(2,), mode="q")
        k_p, kbig, k_s = _proj(xp_b, xs_b, w_l, col_blocks=(3,), mode="kv", big=kbig, layer=l, depth=depth)
        v_p, vbig, v_s = _proj(xp_b, xs_b, w_l, col_blocks=(4,), mode="kv", big=vbig, layer=l, depth=depth)

        wg = _gate_weights(w_ga[l], w_gx[l])
        row = lambda a: a.reshape(1, -1)
        rg_args = (conv_w[l], row(conv_b[l]), wg, row(b_ga[l]), row(b_gx[l]), row(lru_lam[l]))
        mrg_p, ht_p, ct_p = _rglru(ug_p, *rg_args, zero_h, zero_c,
                                   batch=bp, seq=seq, bt=256, out_dtype=BF16)
        mrg_s, ht_s, ct_s = _rglru(ug_s, *rg_args, state_h[l].reshape(bs, 1, d_rg), state_conv[l],
                                   batch=bs, seq=t_new, bt=t_new, out_dtype=F32)

        da_args = (row(subln_g[l]), row(lq1[l]), row(lk1[l]), row(lq2[l]), row(lk2[l]))
        mda_p = _attn_prompt(slopes, q_p, k_p, v_p, ug_p, *da_args,
                             batch=bp, seq=seq, heads=heads, lam_init=lam_init)
        k_s3 = k_s.reshape(bs * t_new, heads, LANES)
        v_s3 = v_s.reshape(bs * t_new, heads, LANES)
        mda_s = _attn_decode(page_table, q_s, k_s3, v_s3, ug_s, *da_args, cache_k, cache_v,
                             layer=l, lam_init=lam_init)

        w_o = w_out[l].astype(BF16)
        xp, xp_b = _out_proj(mrg_p, mda_p, w_o, xp, row(ln_g[l]), row(ln_b[l]), alpha=alpha, bm=512)
        xs, xs_b = _out_proj(mrg_s, mda_s, w_o, xs, row(ln_g[l]), row(ln_b[l]), alpha=alpha, bm=bs * t_new)

        ks_l.append(k_s3.reshape(bs, t_new, heads, LANES))
        vs_l.append(v_s3.reshape(bs, t_new, heads, LANES))
        hp_l.append(ht_p.reshape(bp, d_rg)); cp_l.append(ct_p)
        hs_l.append(ht_s.reshape(bs, d_rg)); cs_l.append(ct_s)

    return (xp.reshape(bp, seq, d_model), xs.reshape(bs, t_new, d_model),
            kbig.reshape(depth, bp, seq, heads, LANES), vbig.reshape(depth, bp, seq, heads, LANES),
            jnp.stack(hp_l), jnp.stack(cp_l),
            jnp.stack(ks_l), jnp.stack(vs_l), jnp.stack(hs_l), jnp.stack(cs_l))
```

```python
import functools
import math

import jax
import jax.numpy as jnp
from jax import lax
from jax.experimental import pallas as pl
from jax.experimental.pallas import tpu as pltpu

F32 = jnp.float32
BF16 = jnp.bfloat16

H_RG = 16
CONV_W = 4
LRU_C = 8.0
DA_HEAD = 64
DA_VDIM = 2 * DA_HEAD
LN_EPS = 1e-5
SUBLN_EPS = 1e-5
MASKED = -1e30
LOG2_E = math.log2(math.e)
Q_SCALE_LOG2 = DA_HEAD ** -0.5 * LOG2_E
ONES_ROWS = 16

LANES = 128
SUBLANES = 8
MXU_TILE = 256
GATE_TILE = MXU_TILE
V7X_VMEM_BYTES = 64 * 1024 * 1024


def _vmem_limit(nbytes):
    return int(min(V7X_VMEM_BYTES - (8 << 20), nbytes + (16 << 20)))


def _lambda_init(layer):
    return 0.8 - 0.6 * math.exp(-0.3 * layer)


def _diff_lambda(lq1_ref, lk1_ref, lq2_ref, lk2_ref, lam_init):
    a = jnp.sum(lq1_ref[...] * lk1_ref[...], axis=1, keepdims=True)
    b = jnp.sum(lq2_ref[...] * lk2_ref[...], axis=1, keepdims=True)
    return jnp.exp(a) - jnp.exp(b) + lam_init


def _sigmoid(x):
    return 0.5 * jnp.tanh(0.5 * x) + 0.5


def _silu(x):
    return x * _sigmoid(x)


def _fold_rows(x, op, chains=4):
    slabs = [x[i:i + SUBLANES] for i in range(0, x.shape[0], SUBLANES)]
    accs = slabs[:chains]
    for i, s in enumerate(slabs[chains:]):
        accs[i % chains] = op(accs[i % chains], s)
    while len(accs) > 1:
        accs = [op(accs[i], accs[i + 1]) for i in range(0, len(accs), 2)]
    return accs[0]


def _dot_nt(a, b):
    return lax.dot_general(a, b, (((1,), (1,)), ((), ())), preferred_element_type=F32)


def _proj_body(*refs, mode, aliased):
    xp_ref, xs_ref, w_ref = refs[:3]
    outs = refs[3 + (1 if aliased else 0):-1]
    wb_scr = refs[-1]
    os_ref = outs[-1]

    @pl.when(pl.program_id(1) == 0)
    def _():
        wb_scr[...] = w_ref[...].astype(BF16)
        os_ref[...] = jnp.dot(xs_ref[...], wb_scr[...], preferred_element_type=F32)

    acc = jnp.dot(xp_ref[...], wb_scr[...], preferred_element_type=F32)
    if mode == "f32":
        op_ref, os_ref = outs
        op_ref[...] = acc
    elif mode == "q":
        op_ref, os_ref = outs
        op_ref[...] = (acc * Q_SCALE_LOG2).astype(BF16)
    else:
        op_ref, o3_ref, os_ref = outs
        op_ref[...] = acc.astype(BF16)
        for h in range(o3_ref.shape[1]):
            o3_ref[:, h, :] = acc[:, h * LANES:(h + 1) * LANES]


def _proj(xp, xs, w, *, col_blocks, mode, big=None, layer=0, bm=1024, bn=1024):
    depth = w.shape[0]
    n_p, d = xp.shape
    n_s = xs.shape[0]
    ncol = len(col_blocks)
    if col_blocks == (0, 1, 5):
        wcol = lambda j: j + 3 * (j // 2)
    else:
        assert ncol == 1
        c0 = col_blocks[0]
        wcol = lambda j: c0
    in_specs = [
        pl.BlockSpec((bm, d), lambda j, i: (i, 0)),
        pl.BlockSpec((n_s, d), lambda j, i: (0, 0)),
        pl.BlockSpec((None, d, bn), lambda j, i: (layer, 0, wcol(j))),
    ]
    args = [xp, xs, w]
    aliases = {}
    nbytes = 2 * (bm * d * 2 + n_s * d * 2 + d * bn * 4 + n_s * bn * 4) + d * bn * 2
    if mode == "kv":
        heads = bn // LANES
        out_shape = [jax.ShapeDtypeStruct((n_p, bn), BF16),
                     jax.ShapeDtypeStruct((depth, n_p, heads, LANES), F32),
                     jax.ShapeDtypeStruct((n_s, bn), F32)]
        out_specs = [pl.BlockSpec((bm, bn), lambda j, i: (i, 0)),
                     pl.BlockSpec((None, bm, heads, LANES), lambda j, i: (layer, i, 0, 0)),
                     pl.BlockSpec((n_s, bn), lambda j, i: (0, 0))]
        nbytes += 2 * (bm * bn * 2 + bm * bn * 4)
        if big is not None:
            in_specs.append(pl.BlockSpec(memory_space=pl.ANY))
            args.append(big)
            aliases = {3: 1}
    else:
        odt = F32 if mode == "f32" else BF16
        out_shape = [jax.ShapeDtypeStruct((n_p, ncol * bn), odt),
                     jax.ShapeDtypeStruct((n_s, ncol * bn), F32)]
        out_specs = [pl.BlockSpec((bm, bn), lambda j, i: (i, j)),
                     pl.BlockSpec((n_s, bn), lambda j, i: (0, j))]
        nbytes += 2 * bm * bn * 4
    return pl.pallas_call(
        functools.partial(_proj_body, mode=mode, aliased=big is not None),
        grid=(ncol, n_p // bm),
        in_specs=in_specs,
        out_specs=out_specs,
        out_shape=out_shape,
        scratch_shapes=[pltpu.VMEM((d, bn), BF16)],
        input_output_aliases=aliases,
        compiler_params=pltpu.CompilerParams(
            dimension_semantics=("arbitrary", "arbitrary"),
            vmem_limit_bytes=_vmem_limit(nbytes + bm * bn * 4)),
        name="proj_" + mode,
    )(*args)


def _rglru_body(u_ref, g_ref, cw_ref, cb_ref, wg_ref, ba_ref, bx_ref, lam_ref, h0_ref, c0_ref,
                o_ref, ht_ref, ct_ref, ext_scr, a_scr, b_scr, hs_scr, car_scr):
    bt, c = u_ref.shape
    t_blk = pl.program_id(1)
    halo = CONV_W - 1

    @pl.when(t_blk == 0)
    def _():
        ext_scr[SUBLANES - halo:SUBLANES, :] = c0_ref[...]
        car_scr[...] = jnp.broadcast_to(h0_ref[...], car_scr.shape)

    u = u_ref[...]
    ext_scr[SUBLANES:SUBLANES + bt, :] = u
    uc = cb_ref[...] + cw_ref[halo:halo + 1, :] * u
    for j in range(halo):
        lag = halo - j
        uc = uc + cw_ref[j:j + 1, :] * ext_scr[SUBLANES - lag:SUBLANES - lag + bt, :]
    tail = ext_scr[SUBLANES + bt - halo:SUBLANES + bt, :]
    ct_ref[...] = tail
    ext_scr[SUBLANES - halo:SUBLANES, :] = tail

    ucb = uc.astype(BF16)
    ngrp = c // GATE_TILE
    pre_a, pre_x = [], []
    for gi in range(ngrp):
        pre = jnp.dot(ucb[:, gi * GATE_TILE:(gi + 1) * GATE_TILE], wg_ref[gi],
                      preferred_element_type=F32)
        pre_a.append(pre[:, :GATE_TILE])
        pre_x.append(pre[:, GATE_TILE:])
    r = _sigmoid(jnp.concatenate(pre_a, axis=1) + ba_ref[...])
    gate_i = _sigmoid(jnp.concatenate(pre_x, axis=1) + bx_ref[...])
    z = -lam_ref[...]
    softplus = jnp.maximum(z, 0.0) + jnp.log1p(jnp.exp(-jnp.abs(z)))
    log_a = (-LRU_C * r) * softplus
    a = jnp.exp(log_a)
    one_minus_a2 = -jnp.tanh(log_a) * (a * a + 1.0)
    a_scr[...] = a
    b_scr[...] = jnp.sqrt(one_minus_a2) * (gate_i * uc)

    row = lax.broadcasted_iota(jnp.int32, (SUBLANES, c), 0)

    def scan_tile(k, carry):
        sl = pl.ds(pl.multiple_of(k * SUBLANES, SUBLANES), SUBLANES)
        av = a_scr[sl, :]
        bv = b_scr[sl, :]
        for s in (1, 2, 4):
            a_sh = jnp.where(row >= s, pltpu.roll(av, s, 0), 1.0)
            b_sh = jnp.where(row >= s, pltpu.roll(bv, s, 0), 0.0)
            bv = bv + av * b_sh
            av = av * a_sh
        h = bv + av * carry
        hs_scr[sl, :] = h
        return jnp.broadcast_to(h[SUBLANES - 1:SUBLANES, :], h.shape)

    carry = lax.fori_loop(0, bt // SUBLANES, scan_tile, car_scr[...])
    car_scr[...] = carry
    ht_ref[...] = carry[0:1, :]
    o_ref[...] = (hs_scr[...] * _silu(g_ref[...])).astype(o_ref.dtype)


def _rglru(ug, cw, cb, wg, ba, bx, lam, h0, c0, *, batch, seq, bt, out_dtype):
    c = cw.shape[1]
    nt = seq // bt
    halo = CONV_W - 1
    row_blk = lambda b, t: (b * nt + t, 0)
    const2 = lambda b, t: (0, 0)
    in_specs = [
        pl.BlockSpec((bt, c), row_blk),
        pl.BlockSpec((bt, c), lambda b, t: (b * nt + t, 1)),
        pl.BlockSpec(cw.shape, const2),
        pl.BlockSpec(cb.shape, const2),
        pl.BlockSpec(wg.shape, lambda b, t: (0, 0, 0)),
        pl.BlockSpec(ba.shape, const2),
        pl.BlockSpec(bx.shape, const2),
        pl.BlockSpec(lam.shape, const2),
        pl.BlockSpec((None, 1, c), lambda b, t: (b, 0, 0)),
        pl.BlockSpec((None, halo, c), lambda b, t: (b, 0, 0)),
    ]
    out_shape = [jax.ShapeDtypeStruct((batch * seq, c), out_dtype),
                 jax.ShapeDtypeStruct((batch, 1, c), F32),
                 jax.ShapeDtypeStruct((batch, halo, c), F32)]
    out_specs = [pl.BlockSpec((bt, c), row_blk),
                 pl.BlockSpec((None, 1, c), lambda b, t: (b, 0, 0)),
                 pl.BlockSpec((None, halo, c), lambda b, t: (b, 0, 0))]
    scratch = [pltpu.VMEM((bt + SUBLANES, c), F32), pltpu.VMEM((bt, c), F32),
               pltpu.VMEM((bt, c), F32), pltpu.VMEM((bt, c), F32),
               pltpu.VMEM((SUBLANES, c), F32)]
    nbytes = 2 * (2 * bt * c * 4 + bt * c * 4) + 4 * bt * c * 4 + wg.size * 2 * 2
    return pl.pallas_call(
        _rglru_body,
        grid=(batch, nt),
        in_specs=in_specs,
        out_specs=out_specs,
        out_shape=out_shape,
        scratch_shapes=scratch,
        compiler_params=pltpu.CompilerParams(
            dimension_semantics=("arbitrary", "arbitrary"),
            vmem_limit_bytes=_vmem_limit(nbytes + 8 * bt * c * 4)),
        name="rglru",
    )(ug, ug, cw, cb, wg, ba, bx, lam, h0, c0)


def _subln_gate(o, sg_ref, gate, lam_init):
    ms = jnp.mean(o * o, axis=-1, keepdims=True)
    o = o * lax.rsqrt(ms + SUBLN_EPS) * sg_ref[...]
    return (o * (1.0 - lam_init)) * _silu(gate)


def _attn_body(slope_ref, q_ref, k_ref, v_ref, g_ref, sg_ref, lq1_ref, lk1_ref, lq2_ref, lk2_ref,
               o_ref, vt_scr, kb_scr, mask_scr, x_scr, xm_scr, m_scr, acc_scr, *, lam_init):
    blk = q_ref.shape[0]
    nk = vt_scr.shape[0]
    dv = v_ref.shape[1]
    b = pl.program_id(0)
    h = pl.program_id(1)
    qi = pl.program_id(2)
    slope = slope_ref[h] * LOG2_E

    @pl.when(jnp.logical_and(jnp.logical_and(b == 0, h == 0), qi == 0))
    def _():
        key = lax.broadcasted_iota(jnp.int32, (blk, blk), 0)
        qry = lax.broadcasted_iota(jnp.int32, (blk, blk), 1)
        mask_scr[...] = jnp.where(key <= qry, 0.0, MASKED)

    @pl.when(qi == 0)
    def _():
        ones = jnp.ones((ONES_ROWS, blk), BF16)
        for j in range(nk):
            vt_scr[j, :dv, :] = v_ref[j * blk:(j + 1) * blk, :].astype(F32).T.astype(BF16)
            vt_scr[j, dv:, :] = ones
        key = lax.broadcasted_iota(jnp.int32, (blk, LANES), 0)
        kb_scr[...] = slope * key.astype(F32)

    q = q_ref[...]
    lane = lax.broadcasted_iota(jnp.int32, q.shape, 1)
    zero = jnp.zeros_like(q)
    q_maps = (jnp.where(lane < DA_HEAD, q, zero), jnp.where(lane >= DA_HEAD, q, zero))
    acc_scr[...] = jnp.zeros(acc_scr.shape, F32)
    m_scr[...] = jnp.full(m_scr.shape, MASKED, F32)

    def scores(ki, n):
        k = k_ref[ki * blk:(ki + 1) * blk, :]
        kb = kb_scr[...] + slope * float((ki - n) * blk)
        kb = pltpu.repeat(kb, blk // LANES, axis=1)
        if ki == n:
            kb = kb + mask_scr[...]
        for c in range(2):
            x = _dot_nt(k, q_maps[c]) + kb
            x_scr[ki % 2, c] = x
            xm_scr[ki % 2, c] = jnp.max(_fold_rows(x, jnp.maximum), axis=0, keepdims=True)

    def softmax_pv(ki):
        vt = vt_scr[ki]
        for c in range(2):
            m = m_scr[c]
            m_new = jnp.maximum(m, xm_scr[ki % 2, c])
            alpha = jnp.exp2(m - m_new)
            p = jnp.exp2(x_scr[ki % 2, c] - m_new)
            acc_scr[c] = alpha * acc_scr[c] + jnp.dot(vt, p.astype(BF16), preferred_element_type=F32)
            m_scr[c] = m_new

    for n in range(nk):
        @pl.when(qi == n)
        def _(n=n):
            scores(0, n)
            for i in range(n):
                scores(i + 1, n)
                softmax_pv(i)
            softmax_pv(n)

    lam = _diff_lambda(lq1_ref, lk1_ref, lq2_ref, lk2_ref, lam_init)
    inv1 = 1.0 / acc_scr[0, dv:dv + 1, :]
    inv2 = lam / acc_scr[1, dv:dv + 1, :]
    ot = acc_scr[0, :dv, :] * inv1 - acc_scr[1, :dv, :] * inv2
    ms = jnp.mean(ot * ot, axis=0, keepdims=True)
    o = (ot * lax.rsqrt(ms + SUBLN_EPS)).T
    o = (o * sg_ref[...]) * (1.0 - lam_init)
    o_ref[...] = (o * _silu(g_ref[...])).astype(o_ref.dtype)


def _attn_prompt(slopes, q, k, v, ug, sg, lq1, lk1, lq2, lk2, *, batch, seq, heads, lam_init, blk=512):
    nq = seq // blk
    gate_col0 = 2 * (q.shape[1] // LANES)
    qmap = lambda b, h, qi: (b * nq + qi, h)
    kmap = lambda b, h, qi: (b, h)
    small = lambda b, h, qi: (0, 0)
    in_specs = [
        pl.BlockSpec(memory_space=pltpu.SMEM),
        pl.BlockSpec((blk, LANES), qmap),
        pl.BlockSpec((seq, LANES), kmap),
        pl.BlockSpec((seq, LANES), kmap),
        pl.BlockSpec((blk, LANES), lambda b, h, qi: (b * nq + qi, gate_col0 + h)),
        pl.BlockSpec(sg.shape, small),
        pl.BlockSpec(lq1.shape, small), pl.BlockSpec(lk1.shape, small),
        pl.BlockSpec(lq2.shape, small), pl.BlockSpec(lk2.shape, small),
    ]
    return pl.pallas_call(
        functools.partial(_attn_body, lam_init=lam_init),
        grid=(batch, heads, nq),
        in_specs=in_specs,
        out_specs=pl.BlockSpec((blk, LANES), qmap),
        out_shape=jax.ShapeDtypeStruct(q.shape, BF16),
        scratch_shapes=[pltpu.VMEM((nq, DA_VDIM + ONES_ROWS, blk), BF16), pltpu.VMEM((blk, LANES), F32),
                        pltpu.VMEM((blk, blk), F32), pltpu.VMEM((2, 2, blk, blk), F32),
                        pltpu.VMEM((2, 2, 1, blk), F32), pltpu.VMEM((2, 1, blk), F32),
                        pltpu.VMEM((2, DA_VDIM + ONES_ROWS, blk), F32)],
        compiler_params=pltpu.CompilerParams(
            dimension_semantics=("arbitrary", "arbitrary", "arbitrary"),
            vmem_limit_bytes=_vmem_limit(16 * blk * blk * 4)),
        name="attn_prompt",
    )(slopes, q, k, v, ug, sg, lq1, lk1, lq2, lk2)


def _fold_lanes(x, op):
    tiles = [x[:, i:i + LANES] for i in range(0, x.shape[1], LANES)]
    while len(tiles) > 1:
        tiles = [op(tiles[i], tiles[i + 1]) for i in range(0, len(tiles), 2)]
    return tiles[0]


def _decode_body(pt_ref, q_ref, kn_ref, vn_ref, g_ref, sg_ref, lq1_ref, lk1_ref, lq2_ref, lk2_ref,
                 *rest, pages_per_step, past_len, lam_init):
    del pt_ref
    gp = pages_per_step
    k_refs = rest[:gp]
    v_refs = rest[gp:2 * gp]
    o_ref = rest[2 * gp]
    w_scr, c_scr, x_scr, m_scr, l_scr, acc_scr = rest[2 * gp + 1:]
    page, heads, dv = v_refs[0].shape
    t_new = q_ref.shape[0]
    rows = heads * 2 * t_new
    cols = page * heads
    b = pl.program_id(0)
    j = pl.program_id(1)
    nj = pl.num_programs(1)

    def row_slope(shape):
        hq = lax.broadcasted_iota(jnp.int32, shape, 0) // (2 * t_new)
        return jnp.exp2(-(hq + 1).astype(F32)) * LOG2_E

    def bias(ncols, causal):
        r = lax.broadcasted_iota(jnp.int32, (rows, ncols), 0)
        c = lax.broadcasted_iota(jnp.int32, (rows, ncols), 1)
        hq, t = r // (2 * t_new), r % t_new
        s, hk = c // heads, c % heads
        keep = hq == hk
        if causal:
            keep = jnp.logical_and(keep, s <= t)
        return jnp.where(keep, row_slope((rows, ncols)) * (s - t).astype(F32), MASKED)

    @pl.when(jnp.logical_and(b == 0, j == 0))
    def _():
        c_scr[...] = bias(cols, False)

    @pl.when(j == 0)
    def _():
        q = q_ref[...]
        lane = lax.broadcasted_iota(jnp.int32, (t_new, LANES), 1)
        parts = []
        for h in range(heads):
            qh = q[:, h * LANES:(h + 1) * LANES] * Q_SCALE_LOG2
            parts.append(jnp.where(lane < DA_HEAD, qh, 0.0))
            parts.append(jnp.where(lane >= DA_HEAD, qh, 0.0))
        w_scr[...] = jnp.concatenate(parts, axis=0).astype(BF16)
        m_scr[...] = jnp.full(m_scr.shape, MASKED, F32)
        l_scr[...] = jnp.zeros(l_scr.shape, F32)
        acc_scr[...] = jnp.zeros(acc_scr.shape, F32)

    w = w_scr[...]
    slope_col = row_slope((rows, 1))
    offs = [slope_col * ((j * gp + gi) * page - past_len).astype(F32) for gi in range(gp)]

    top = None
    for gi in range(gp):
        page_top = None
        for c0 in range(0, cols, MXU_TILE):
            kp = k_refs[gi][c0 // heads:(c0 + MXU_TILE) // heads].reshape(MXU_TILE, LANES).astype(BF16)
            x = _dot_nt(w, kp) + c_scr[:, c0:c0 + MXU_TILE]
            x_scr[gi, :, c0:c0 + MXU_TILE] = x
            tile_top = _fold_lanes(x, jnp.maximum)
            page_top = tile_top if page_top is None else jnp.maximum(page_top, tile_top)
        page_top = page_top + offs[gi]
        top = page_top if top is None else jnp.maximum(top, page_top)
    m_old = m_scr[...]
    m_new = jnp.maximum(m_old, jnp.max(top, axis=1, keepdims=True))
    m_scr[...] = m_new

    pv = None
    psum = None
    for gi in range(gp):
        p = jnp.exp2(x_scr[gi] - (m_new - offs[gi]))
        vp = v_refs[gi][...].reshape(cols, LANES).astype(BF16)
        part = jnp.dot(p.astype(BF16), vp, preferred_element_type=F32)
        part_sum = _fold_lanes(p, jnp.add)
        pv = part if pv is None else pv + part
        psum = part_sum if psum is None else psum + part_sum
    alpha = jnp.exp2(m_old - m_new)
    acc = alpha * acc_scr[...] + pv
    l = alpha * l_scr[...] + jnp.sum(psum, axis=1, keepdims=True)
    acc_scr[...] = acc
    l_scr[...] = l

    @pl.when(j == nj - 1)
    def _():
        kn = kn_ref[...].reshape(t_new * heads, LANES).astype(BF16)
        vn = vn_ref[...].reshape(t_new * heads, LANES).astype(BF16)
        x = _dot_nt(w, kn) + bias(t_new * heads, True)
        m_fin = jnp.maximum(m_new, jnp.max(x, axis=1, keepdims=True))
        p = jnp.exp2(x - m_fin)
        beta = jnp.exp2(m_new - m_fin)
        acc2 = beta * acc + jnp.dot(p.astype(BF16), vn, preferred_element_type=F32)
        o = acc2 / (beta * l + jnp.sum(p, axis=1, keepdims=True))
        lam = _diff_lambda(lq1_ref, lk1_ref, lq2_ref, lk2_ref, lam_init)
        for h in range(heads):
            r0 = h * 2 * t_new
            oh = o[r0:r0 + t_new] - lam * o[r0 + t_new:r0 + 2 * t_new]
            gate = g_ref[:, h * LANES:(h + 1) * LANES]
            o_ref[:, h * LANES:(h + 1) * LANES] = _subln_gate(oh, sg_ref, gate, lam_init)


def _attn_decode(page_table, q, kn, vn, ug, sg, lq1, lk1, lq2, lk2, cache_k, cache_v, *,
                 layer, lam_init, pages_per_step=8):
    batch, n_pages = page_table.shape
    _, _, page, heads, _ = cache_k.shape
    t_new = q.shape[0] // batch
    width = q.shape[1]
    gp = pages_per_step
    rows = heads * 2 * t_new
    cols = page * heads
    small = lambda b, j, pt: (0, 0)

    def page_spec(gi):
        return pl.BlockSpec((None, None, page, heads, LANES),
                            lambda b, j, pt: (layer, pt[b, j * gp + gi], 0, 0, 0))

    in_specs = [
        pl.BlockSpec((t_new, width), lambda b, j, pt: (b, 0)),
        pl.BlockSpec((t_new, heads, LANES), lambda b, j, pt: (b, 0, 0)),
        pl.BlockSpec((t_new, heads, LANES), lambda b, j, pt: (b, 0, 0)),
        pl.BlockSpec((t_new, width), lambda b, j, pt: (b, 2)),
        pl.BlockSpec(sg.shape, small),
        pl.BlockSpec(lq1.shape, small), pl.BlockSpec(lk1.shape, small),
        pl.BlockSpec(lq2.shape, small), pl.BlockSpec(lk2.shape, small),
    ] + [page_spec(gi) for gi in range(gp)] * 2
    grid_spec = pltpu.PrefetchScalarGridSpec(
        num_scalar_prefetch=1,
        grid=(batch, n_pages // gp),
        in_specs=in_specs,
        out_specs=pl.BlockSpec((t_new, width), lambda b, j, pt: (b, 0)),
        scratch_shapes=[pltpu.VMEM((rows, LANES), BF16), pltpu.VMEM((rows, cols), F32),
                        pltpu.VMEM((gp, rows, cols), F32), pltpu.VMEM((rows, 1), F32),
                        pltpu.VMEM((rows, 1), F32), pltpu.VMEM((rows, DA_VDIM), F32)],
    )
    page_bytes = page * heads * LANES * 4
    return pl.pallas_call(
        functools.partial(_decode_body, pages_per_step=gp, past_len=n_pages * page, lam_init=lam_init),
        grid_spec=grid_spec,
        out_shape=jax.ShapeDtypeStruct(q.shape, F32),
        compiler_params=pltpu.CompilerParams(
            dimension_semantics=("arbitrary", "arbitrary"),
            vmem_limit_bytes=_vmem_limit(4 * gp * page_bytes + (gp + 4) * rows * cols * 4)),
        name="attn_decode",
    )(page_table, q, kn, vn, ug, sg, lq1, lk1, lq2, lk2, *([cache_k] * gp), *([cache_v] * gp))


def _out_body(mrg_ref, mda_ref, w_ref, x_ref, g_ref, b_ref, of_ref, ob_ref, *, alpha):
    half = mrg_ref.shape[1]
    y = jnp.dot(mrg_ref[...].astype(BF16), w_ref[:half, :], preferred_element_type=F32)
    y = y + jnp.dot(mda_ref[...].astype(BF16), w_ref[half:, :], preferred_element_type=F32)
    z = alpha * x_ref[...] + y
    mu = jnp.mean(z, axis=-1, keepdims=True)
    zc = z - mu
    var = jnp.mean(zc * zc, axis=-1, keepdims=True)
    out = zc * lax.rsqrt(var + LN_EPS) * g_ref[...] + b_ref[...]
    of_ref[...] = out
    ob_ref[...] = out.astype(BF16)


def _out_proj(mrg, mda, w, x, g, b, *, layer, alpha, bm):
    n, d = x.shape
    half = mrg.shape[1]
    rowb = lambda i: (i, 0)
    const = lambda i: (0, 0)
    nbytes = 2 * (2 * bm * half * 4 + w[0].size * 2 + bm * d * 4 + bm * d * 6)
    return pl.pallas_call(
        functools.partial(_out_body, alpha=alpha),
        grid=(n // bm,),
        in_specs=[pl.BlockSpec((bm, half), rowb), pl.BlockSpec((bm, half), rowb),
                  pl.BlockSpec((None,) + w.shape[1:], lambda i: (layer, 0, 0)),
                  pl.BlockSpec((bm, d), rowb),
                  pl.BlockSpec(g.shape, const), pl.BlockSpec(b.shape, const)],
        out_specs=[pl.BlockSpec((bm, d), rowb), pl.BlockSpec((bm, d), rowb)],
        out_shape=[jax.ShapeDtypeStruct((n, d), F32), jax.ShapeDtypeStruct((n, d), BF16)],
        compiler_params=pltpu.CompilerParams(
            dimension_semantics=("arbitrary",),
            vmem_limit_bytes=_vmem_limit(nbytes + 4 * bm * d * 4)),
        name="out_proj",
    )(mrg, mda, w, x, g, b)


def _gate_weights(w_a, w_x):
    per = GATE_TILE // w_a.shape[1]

    def tiles(w):
        w4 = w.reshape(-1, per, w.shape[1], w.shape[2])
        eye = jnp.eye(per, dtype=w.dtype)
        return jnp.einsum("gncd,nm->gncmd", w4, eye).reshape(-1, GATE_TILE, GATE_TILE)

    return jnp.concatenate([tiles(w_a), tiles(w_x)], axis=2).astype(BF16)


def kernel(x_prompt, x_sample, cache_k, cache_v, state_h, state_conv, page_table, w_in, conv_w, conv_b,
           w_ga, b_ga, w_gx, b_gx, lru_lam, lq1, lk1, lq2, lk2, subln_g, w_out, ln_g, ln_b):
    depth = w_in.shape[0]
    bp, seq, d_model = x_prompt.shape
    bs, t_new, _ = x_sample.shape
    heads = cache_k.shape[3]
    d_rg = conv_w.shape[2]
    alpha = (2 * depth) ** 0.25
    slopes = jnp.exp2(-8.0 * jnp.arange(1, heads + 1, dtype=F32) / heads)

    xp = x_prompt.reshape(bp * seq, d_model)
    xs = x_sample.reshape(bs * t_new, d_model)
    xp_b, xs_b = xp.astype(BF16), xs.astype(BF16)
    w_out_b = w_out.astype(BF16)
    zero_h = jnp.zeros((bp, 1, d_rg), F32)
    zero_c = jnp.zeros((bp, CONV_W - 1, d_rg), F32)

    kbig = vbig = None
    ks_l, vs_l, hp_l, cp_l, hs_l, cs_l = [], [], [], [], [], []
    for l in range(depth):
        lam_init = _lambda_init(l)
        ug_p, ug_s = _proj(xp_b, xs_b, w_in, col_blocks=(0, 1, 5), mode="f32", layer=l)
        q_p, q_s = _proj(xp_b, xs_b, w_in, col_blocks=(2,), mode="q", layer=l)
        k_p, kbig, k_s = _proj(xp_b, xs_b, w_in, col_blocks=(3,), mode="kv", big=kbig, layer=l)
        v_p, vbig, v_s = _proj(xp_b, xs_b, w_in, col_blocks=(4,), mode="kv", big=vbig, layer=l)

        wg = _gate_weights(w_ga[l], w_gx[l])
        row = lambda a: a.reshape(1, -1)
        rg_args = (conv_w[l], row(conv_b[l]), wg, row(b_ga[l]), row(b_gx[l]), row(lru_lam[l]))
        mrg_p, ht_p, ct_p = _rglru(ug_p, *rg_args, zero_h, zero_c,
                                   batch=bp, seq=seq, bt=256, out_dtype=BF16)
        mrg_s, ht_s, ct_s = _rglru(ug_s, *rg_args, state_h[l].reshape(bs, 1, d_rg), state_conv[l],
                                   batch=bs, seq=t_new, bt=t_new, out_dtype=F32)

        da_args = (row(subln_g[l]), row(lq1[l]), row(lk1[l]), row(lq2[l]), row(lk2[l]))
        mda_p = _attn_prompt(slopes, q_p, k_p, v_p, ug_p, *da_args,
                             batch=bp, seq=seq, heads=heads, lam_init=lam_init)
        k_s3 = k_s.reshape(bs * t_new, heads, LANES)
        v_s3 = v_s.reshape(bs * t_new, heads, LANES)
        mda_s = _attn_decode(page_table, q_s, k_s3, v_s3, ug_s, *da_args, cache_k, cache_v,
                             layer=l, lam_init=lam_init)

        ln = (row(ln_g[l]), row(ln_b[l]))
        xp, xp_b = _out_proj(mrg_p, mda_p, w_out_b, xp, *ln, layer=l, alpha=alpha, bm=512)
        xs, xs_b = _out_proj(mrg_s, mda_s, w_out_b, xs, *ln, layer=l, alpha=alpha, bm=bs * t_new)

        ks_l.append(k_s3.reshape(bs, t_new, heads, LANES))
        vs_l.append(v_s3.reshape(bs, t_new, heads, LANES))
        hp_l.append(ht_p.reshape(bp, d_rg)); cp_l.append(ct_p)
        hs_l.append(ht_s.reshape(bs, d_rg)); cs_l.append(ct_s)

    return (xp.reshape(bp, seq, d_model), xs.reshape(bs, t_new, d_model),
            kbig.reshape(depth, bp, seq, heads, LANES), vbig.reshape(depth, bp, seq, heads, LANES),
            jnp.stack(hp_l), jnp.stack(cp_l),
            jnp.stack(ks_l), jnp.stack(vs_l), jnp.stack(hs_l), jnp.stack(cs_l))
```

```python
import functools
import math

import jax
import jax.numpy as jnp
from jax import lax
from jax.experimental import pallas as pl
from jax.experimental.pallas import tpu as pltpu

F32 = jnp.float32
BF16 = jnp.bfloat16

H_RG = 16
CONV_W = 4
LRU_C = 8.0
DA_HEAD = 64
DA_VDIM = 2 * DA_HEAD
LN_EPS = 1e-5
SUBLN_EPS = 1e-5
MASKED = -1e30
LOG2_E = math.log2(math.e)
Q_SCALE_LOG2 = DA_HEAD ** -0.5 * LOG2_E
ONES_ROWS = 16

LANES = 128
SUBLANES = 8
MXU_TILE = 256
GATE_TILE = MXU_TILE
V7X_VMEM_BYTES = 64 * 1024 * 1024


def _vmem_limit(nbytes):
    return int(min(V7X_VMEM_BYTES - (8 << 20), nbytes + (16 << 20)))


def _lambda_init(layer):
    return 0.8 - 0.6 * math.exp(-0.3 * layer)


def _diff_lambda(lq1_ref, lk1_ref, lq2_ref, lk2_ref, lam_init):
    a = jnp.sum(lq1_ref[...] * lk1_ref[...], axis=1, keepdims=True)
    b = jnp.sum(lq2_ref[...] * lk2_ref[...], axis=1, keepdims=True)
    return jnp.exp(a) - jnp.exp(b) + lam_init


def _sigmoid(x):
    return 0.5 * jnp.tanh(0.5 * x) + 0.5


def _silu(x):
    return x * _sigmoid(x)


def _fold_rows(x, op, chains=4):
    slabs = [x[i:i + SUBLANES] for i in range(0, x.shape[0], SUBLANES)]
    accs = slabs[:chains]
    for i, s in enumerate(slabs[chains:]):
        accs[i % chains] = op(accs[i % chains], s)
    while len(accs) > 1:
        accs = [op(accs[i], accs[i + 1]) for i in range(0, len(accs), 2)]
    return accs[0]


def _dot_nt(a, b):
    return lax.dot_general(a, b, (((1,), (1,)), ((), ())), preferred_element_type=F32)


def _proj_body(*refs, mode, aliased):
    xp_ref, xs_ref, w_ref = refs[:3]
    outs = refs[3 + (1 if aliased else 0):-1]
    wb_scr = refs[-1]
    os_ref = outs[-1]

    @pl.when(pl.program_id(1) == 0)
    def _():
        wb_scr[...] = w_ref[...].astype(BF16)
        os_ref[...] = jnp.dot(xs_ref[...], wb_scr[...], preferred_element_type=F32)

    acc = jnp.dot(xp_ref[...], wb_scr[...], preferred_element_type=F32)
    if mode == "f32":
        op_ref, os_ref = outs
        op_ref[...] = acc
    elif mode == "q":
        op_ref, os_ref = outs
        op_ref[...] = (acc * Q_SCALE_LOG2).astype(BF16)
    else:
        op_ref, o3_ref, os_ref = outs
        op_ref[...] = acc.astype(BF16)
        for h in range(o3_ref.shape[1]):
            o3_ref[:, h, :] = acc[:, h * LANES:(h + 1) * LANES]


def _proj(xp, xs, w, *, col_blocks, mode, big=None, layer=0, bm=1024, bn=1024):
    depth = w.shape[0]
    n_p, d = xp.shape
    n_s = xs.shape[0]
    ncol = len(col_blocks)
    if col_blocks == (0, 1, 5):
        wcol = lambda j: j + 3 * (j // 2)
    else:
        assert ncol == 1
        c0 = col_blocks[0]
        wcol = lambda j: c0
    in_specs = [
        pl.BlockSpec((bm, d), lambda j, i: (i, 0)),
        pl.BlockSpec((n_s, d), lambda j, i: (0, 0)),
        pl.BlockSpec((None, d, bn), lambda j, i: (layer, 0, wcol(j))),
    ]
    args = [xp, xs, w]
    aliases = {}
    nbytes = 2 * (bm * d * 2 + n_s * d * 2 + d * bn * 4 + n_s * bn * 4) + d * bn * 2
    if mode == "kv":
        heads = bn // LANES
        out_shape = [jax.ShapeDtypeStruct((n_p, bn), BF16),
                     jax.ShapeDtypeStruct((depth, n_p, heads, LANES), F32),
                     jax.ShapeDtypeStruct((n_s, bn), F32)]
        out_specs = [pl.BlockSpec((bm, bn), lambda j, i: (i, 0)),
                     pl.BlockSpec((None, bm, heads, LANES), lambda j, i: (layer, i, 0, 0)),
                     pl.BlockSpec((n_s, bn), lambda j, i: (0, 0))]
        nbytes += 2 * (bm * bn * 2 + bm * bn * 4)
        if big is not None:
            in_specs.append(pl.BlockSpec(memory_space=pl.ANY))
            args.append(big)
            aliases = {3: 1}
    else:
        odt = F32 if mode == "f32" else BF16
        out_shape = [jax.ShapeDtypeStruct((n_p, ncol * bn), odt),
                     jax.ShapeDtypeStruct((n_s, ncol * bn), F32)]
        out_specs = [pl.BlockSpec((bm, bn), lambda j, i: (i, j)),
                     pl.BlockSpec((n_s, bn), lambda j, i: (0, j))]
        nbytes += 2 * bm * bn * 4
    return pl.pallas_call(
        functools.partial(_proj_body, mode=mode, aliased=big is not None),
        grid=(ncol, n_p // bm),
        in_specs=in_specs,
        out_specs=out_specs,
        out_shape=out_shape,
        scratch_shapes=[pltpu.VMEM((d, bn), BF16)],
        input_output_aliases=aliases,
        compiler_params=pltpu.CompilerParams(
            dimension_semantics=("arbitrary", "arbitrary"),
            vmem_limit_bytes=_vmem_limit(nbytes + bm * bn * 4)),
        name="proj_" + mode,
    )(*args)


def _rglru_body(u_ref, g_ref, cw_ref, cb_ref, wg_ref, ba_ref, bx_ref, lam_ref, h0_ref, c0_ref,
                o_ref, ht_ref, ct_ref, ext_scr, a_scr, b_scr, hs_scr, car_scr):
    bt, c = u_ref.shape
    t_blk = pl.program_id(1)
    halo = CONV_W - 1

    @pl.when(t_blk == 0)
    def _():
        ext_scr[SUBLANES - halo:SUBLANES, :] = c0_ref[...]
        car_scr[...] = jnp.broadcast_to(h0_ref[...], car_scr.shape)

    u = u_ref[...]
    ext_scr[SUBLANES:SUBLANES + bt, :] = u
    uc = cb_ref[...] + cw_ref[halo:halo + 1, :] * u
    for j in range(halo):
        lag = halo - j
        uc = uc + cw_ref[j:j + 1, :] * ext_scr[SUBLANES - lag:SUBLANES - lag + bt, :]
    tail = ext_scr[SUBLANES + bt - halo:SUBLANES + bt, :]
    ct_ref[...] = tail
    ext_scr[SUBLANES - halo:SUBLANES, :] = tail

    ucb = uc.astype(BF16)
    ngrp = c // GATE_TILE
    pre_a, pre_x = [], []
    for gi in range(ngrp):
        pre = jnp.dot(ucb[:, gi * GATE_TILE:(gi + 1) * GATE_TILE], wg_ref[gi],
                      preferred_element_type=F32)
        pre_a.append(pre[:, :GATE_TILE])
        pre_x.append(pre[:, GATE_TILE:])
    r = _sigmoid(jnp.concatenate(pre_a, axis=1) + ba_ref[...])
    gate_i = _sigmoid(jnp.concatenate(pre_x, axis=1) + bx_ref[...])
    z = -lam_ref[...]
    softplus = jnp.maximum(z, 0.0) + jnp.log1p(jnp.exp(-jnp.abs(z)))
    log_a = (-LRU_C * r) * softplus
    a = jnp.exp(log_a)
    one_minus_a2 = -jnp.tanh(log_a) * (a * a + 1.0)
    a_scr[...] = a
    b_scr[...] = jnp.sqrt(one_minus_a2) * (gate_i * uc)

    row = lax.broadcasted_iota(jnp.int32, (SUBLANES, c), 0)

    def scan_tile(k, carry):
        sl = pl.ds(pl.multiple_of(k * SUBLANES, SUBLANES), SUBLANES)
        av = a_scr[sl, :]
        bv = b_scr[sl, :]
        for s in (1, 2, 4):
            a_sh = jnp.where(row >= s, pltpu.roll(av, s, 0), 1.0)
            b_sh = jnp.where(row >= s, pltpu.roll(bv, s, 0), 0.0)
            bv = bv + av * b_sh
            av = av * a_sh
        h = bv + av * carry
        hs_scr[sl, :] = h
        return jnp.broadcast_to(h[SUBLANES - 1:SUBLANES, :], h.shape)

    carry = lax.fori_loop(0, bt // SUBLANES, scan_tile, car_scr[...])
    car_scr[...] = carry
    ht_ref[...] = carry[0:1, :]
    o_ref[...] = (hs_scr[...] * _silu(g_ref[...])).astype(o_ref.dtype)


def _rglru(ug, cw, cb, wg, ba, bx, lam, h0, c0, *, batch, seq, bt, out_dtype):
    c = cw.shape[1]
    nt = seq // bt
    halo = CONV_W - 1
    row_blk = lambda b, t: (b * nt + t, 0)
    const2 = lambda b, t: (0, 0)
    in_specs = [
        pl.BlockSpec((bt, c), row_blk),
        pl.BlockSpec((bt, c), lambda b, t: (b * nt + t, 1)),
        pl.BlockSpec(cw.shape, const2),
        pl.BlockSpec(cb.shape, const2),
        pl.BlockSpec(wg.shape, lambda b, t: (0, 0, 0)),
        pl.BlockSpec(ba.shape, const2),
        pl.BlockSpec(bx.shape, const2),
        pl.BlockSpec(lam.shape, const2),
        pl.BlockSpec((None, 1, c), lambda b, t: (b, 0, 0)),
        pl.BlockSpec((None, halo, c), lambda b, t: (b, 0, 0)),
    ]
    out_shape = [jax.ShapeDtypeStruct((batch * seq, c), out_dtype),
                 jax.ShapeDtypeStruct((batch, 1, c), F32),
                 jax.ShapeDtypeStruct((batch, halo, c), F32)]
    out_specs = [pl.BlockSpec((bt, c), row_blk),
                 pl.BlockSpec((None, 1, c), lambda b, t: (b, 0, 0)),
                 pl.BlockSpec((None, halo, c), lambda b, t: (b, 0, 0))]
    scratch = [pltpu.VMEM((bt + SUBLANES, c), F32), pltpu.VMEM((bt, c), F32),
               pltpu.VMEM((bt, c), F32), pltpu.VMEM((bt, c), F32),
               pltpu.VMEM((SUBLANES, c), F32)]
    nbytes = 2 * (2 * bt * c * 4 + bt * c * 4) + 4 * bt * c * 4 + wg.size * 2 * 2
    return pl.pallas_call(
        _rglru_body,
        grid=(batch, nt),
        in_specs=in_specs,
        out_specs=out_specs,
        out_shape=out_shape,
        scratch_shapes=scratch,
        compiler_params=pltpu.CompilerParams(
            dimension_semantics=("arbitrary", "arbitrary"),
            vmem_limit_bytes=_vmem_limit(nbytes + 8 * bt * c * 4)),
        name="rglru",
    )(ug, ug, cw, cb, wg, ba, bx, lam, h0, c0)


def _subln_gate(o, sg_ref, gate, lam_init):
    ms = jnp.mean(o * o, axis=-1, keepdims=True)
    o = o * lax.rsqrt(ms + SUBLN_EPS) * sg_ref[...]
    return (o * (1.0 - lam_init)) * _silu(gate)


def _attn_body(slope_ref, q_ref, k_ref, v_ref, g_ref, sg_ref, lq1_ref, lk1_ref, lq2_ref, lk2_ref,
               o_ref, vt_scr, kb_scr, mask_scr, x_scr, xm_scr, m_scr, acc_scr, *, lam_init):
    blk = q_ref.shape[0]
    nk = vt_scr.shape[0]
    dv = v_ref.shape[1]
    b = pl.program_id(0)
    h = pl.program_id(1)
    qi = pl.program_id(2)
    slope = slope_ref[h] * LOG2_E

    @pl.when(jnp.logical_and(jnp.logical_and(b == 0, h == 0), qi == 0))
    def _():
        key = lax.broadcasted_iota(jnp.int32, (blk, blk), 0)
        qry = lax.broadcasted_iota(jnp.int32, (blk, blk), 1)
        mask_scr[...] = jnp.where(key <= qry, 0.0, MASKED)

    @pl.when(qi == 0)
    def _():
        ones = jnp.ones((ONES_ROWS, blk), BF16)
        for j in range(nk):
            vt_scr[j, :dv, :] = v_ref[j * blk:(j + 1) * blk, :].astype(F32).T.astype(BF16)
            vt_scr[j, dv:, :] = ones
        key = lax.broadcasted_iota(jnp.int32, (blk, LANES), 0)
        kb_scr[...] = slope * key.astype(F32)

    q = q_ref[...]
    lane = lax.broadcasted_iota(jnp.int32, q.shape, 1)
    zero = jnp.zeros_like(q)
    q_maps = (jnp.where(lane < DA_HEAD, q, zero), jnp.where(lane >= DA_HEAD, q, zero))
    acc_scr[...] = jnp.zeros(acc_scr.shape, F32)
    m_scr[...] = jnp.full(m_scr.shape, MASKED, F32)

    def parts(ki, n):
        if ki < n:
            return [((0, blk), (0, blk))]
        return [((0, blk // 2), (0, blk // 2)), ((0, blk), (blk // 2, blk))]

    def scores(ki, n):
        for (k0, k1), (q0, q1) in parts(ki, n):
            k = k_ref[ki * blk + k0:ki * blk + k1, :]
            kb = kb_scr[k0:k1, :] + slope * float((ki - n) * blk)
            kb = pltpu.repeat(kb, (q1 - q0) // LANES, axis=1)
            if ki == n:
                kb = kb + mask_scr[k0:k1, q0:q1]
            for c in range(2):
                x = _dot_nt(k, q_maps[c][q0:q1]) + kb
                x_scr[ki % 2, c, k0:k1, q0:q1] = x
                xm_scr[ki % 2, c, :, q0:q1] = jnp.max(_fold_rows(x, jnp.maximum), axis=0, keepdims=True)

    def softmax_pv(ki, n):
        for (k0, k1), (q0, q1) in parts(ki, n):
            vt = vt_scr[ki, :, k0:k1]
            for c in range(2):
                m = m_scr[c, :, q0:q1]
                m_new = jnp.maximum(m, xm_scr[ki % 2, c, :, q0:q1])
                alpha = jnp.exp2(m - m_new)
                p = jnp.exp2(x_scr[ki % 2, c, k0:k1, q0:q1] - m_new)
                acc_scr[c, :, q0:q1] = alpha * acc_scr[c, :, q0:q1] + jnp.dot(
                    vt, p.astype(BF16), preferred_element_type=F32)
                m_scr[c, :, q0:q1] = m_new

    for n in range(nk):
        @pl.when(qi == n)
        def _(n=n):
            scores(0, n)
            for i in range(n):
                scores(i + 1, n)
                softmax_pv(i, n)
            softmax_pv(n, n)

    lam = _diff_lambda(lq1_ref, lk1_ref, lq2_ref, lk2_ref, lam_init)
    inv1 = 1.0 / acc_scr[0, dv:dv + 1, :]
    inv2 = lam / acc_scr[1, dv:dv + 1, :]
    ot = acc_scr[0, :dv, :] * inv1 - acc_scr[1, :dv, :] * inv2
    ms = jnp.mean(ot * ot, axis=0, keepdims=True)
    o = (ot * lax.rsqrt(ms + SUBLN_EPS)).T
    o = (o * sg_ref[...]) * (1.0 - lam_init)
    o_ref[...] = (o * _silu(g_ref[...])).astype(o_ref.dtype)


def _attn_prompt(slopes, q, k, v, ug, sg, lq1, lk1, lq2, lk2, *, batch, seq, heads, lam_init, blk=512):
    nq = seq // blk
    gate_col0 = 2 * (q.shape[1] // LANES)
    qmap = lambda b, h, qi: (b * nq + qi, h)
    kmap = lambda b, h, qi: (b, h)
    small = lambda b, h, qi: (0, 0)
    in_specs = [
        pl.BlockSpec(memory_space=pltpu.SMEM),
        pl.BlockSpec((blk, LANES), qmap),
        pl.BlockSpec((seq, LANES), kmap),
        pl.BlockSpec((seq, LANES), kmap),
        pl.BlockSpec((blk, LANES), lambda b, h, qi: (b * nq + qi, gate_col0 + h)),
        pl.BlockSpec(sg.shape, small),
        pl.BlockSpec(lq1.shape, small), pl.BlockSpec(lk1.shape, small),
        pl.BlockSpec(lq2.shape, small), pl.BlockSpec(lk2.shape, small),
    ]
    return pl.pallas_call(
        functools.partial(_attn_body, lam_init=lam_init),
        grid=(batch, heads, nq),
        in_specs=in_specs,
        out_specs=pl.BlockSpec((blk, LANES), qmap),
        out_shape=jax.ShapeDtypeStruct(q.shape, BF16),
        scratch_shapes=[pltpu.VMEM((nq, DA_VDIM + ONES_ROWS, blk), BF16), pltpu.VMEM((blk, LANES), F32),
                        pltpu.VMEM((blk, blk), F32), pltpu.VMEM((2, 2, blk, blk), F32),
                        pltpu.VMEM((2, 2, 1, blk), F32), pltpu.VMEM((2, 1, blk), F32),
                        pltpu.VMEM((2, DA_VDIM + ONES_ROWS, blk), F32)],
        compiler_params=pltpu.CompilerParams(
            dimension_semantics=("arbitrary", "arbitrary", "arbitrary"),
            vmem_limit_bytes=_vmem_limit(16 * blk * blk * 4)),
        name="attn_prompt",
    )(slopes, q, k, v, ug, sg, lq1, lk1, lq2, lk2)


def _fold_lanes(x, op):
    tiles = [x[:, i:i + LANES] for i in range(0, x.shape[1], LANES)]
    while len(tiles) > 1:
        tiles = [op(tiles[i], tiles[i + 1]) for i in range(0, len(tiles), 2)]
    return tiles[0]


def _decode_body(pt_ref, q_ref, kn_ref, vn_ref, g_ref, sg_ref, lq1_ref, lk1_ref, lq2_ref, lk2_ref,
                 *rest, pages_per_step, past_len, lam_init):
    del pt_ref
    gp = pages_per_step
    k_refs = rest[:gp]
    v_refs = rest[gp:2 * gp]
    o_ref = rest[2 * gp]
    w_scr, c_scr, x_scr, m_scr, l_scr, acc_scr = rest[2 * gp + 1:]
    page, heads, dv = v_refs[0].shape
    t_new = q_ref.shape[0]
    rows = heads * 2 * t_new
    cols = page * heads
    b = pl.program_id(0)
    j = pl.program_id(1)
    nj = pl.num_programs(1)

    def row_slope(shape):
        hq = lax.broadcasted_iota(jnp.int32, shape, 0) // (2 * t_new)
        return jnp.exp2(-(hq + 1).astype(F32)) * LOG2_E

    def bias(ncols, causal):
        r = lax.broadcasted_iota(jnp.int32, (rows, ncols), 0)
        c = lax.broadcasted_iota(jnp.int32, (rows, ncols), 1)
        hq, t = r // (2 * t_new), r % t_new
        s, hk = c // heads, c % heads
        keep = hq == hk
        if causal:
            keep = jnp.logical_and(keep, s <= t)
        return jnp.where(keep, row_slope((rows, ncols)) * (s - t).astype(F32), MASKED)

    @pl.when(jnp.logical_and(b == 0, j == 0))
    def _():
        c_scr[...] = bias(cols, False)

    @pl.when(j == 0)
    def _():
        q = q_ref[...]
        lane = lax.broadcasted_iota(jnp.int32, (t_new, LANES), 1)
        parts = []
        for h in range(heads):
            qh = q[:, h * LANES:(h + 1) * LANES] * Q_SCALE_LOG2
            parts.append(jnp.where(lane < DA_HEAD, qh, 0.0))
            parts.append(jnp.where(lane >= DA_HEAD, qh, 0.0))
        w_scr[...] = jnp.concatenate(parts, axis=0).astype(BF16)
        m_scr[...] = jnp.full(m_scr.shape, MASKED, F32)
        l_scr[...] = jnp.zeros(l_scr.shape, F32)
        acc_scr[...] = jnp.zeros(acc_scr.shape, F32)

    w = w_scr[...]
    slope_col = row_slope((rows, 1))
    offs = [slope_col * ((j * gp + gi) * page - past_len).astype(F32) for gi in range(gp)]

    top = None
    for gi in range(gp):
        page_top = None
        for c0 in range(0, cols, MXU_TILE):
            kp = k_refs[gi][c0 // heads:(c0 + MXU_TILE) // heads].reshape(MXU_TILE, LANES).astype(BF16)
            x = _dot_nt(w, kp) + c_scr[:, c0:c0 + MXU_TILE]
            x_scr[gi, :, c0:c0 + MXU_TILE] = x
            tile_top = _fold_lanes(x, jnp.maximum)
            page_top = tile_top if page_top is None else jnp.maximum(page_top, tile_top)
        page_top = page_top + offs[gi]
        top = page_top if top is None else jnp.maximum(top, page_top)
    m_old = m_scr[...]
    m_new = jnp.maximum(m_old, jnp.max(top, axis=1, keepdims=True))
    m_scr[...] = m_new

    pv = None
    psum = None
    for gi in range(gp):
        p = jnp.exp2(x_scr[gi] - (m_new - offs[gi]))
        vp = v_refs[gi][...].reshape(cols, LANES).astype(BF16)
        part = jnp.dot(p.astype(BF16), vp, preferred_element_type=F32)
        part_sum = _fold_lanes(p, jnp.add)
        pv = part if pv is None else pv + part
        psum = part_sum if psum is None else psum + part_sum
    alpha = jnp.exp2(m_old - m_new)
    acc = alpha * acc_scr[...] + pv
    l = alpha * l_scr[...] + jnp.sum(psum, axis=1, keepdims=True)
    acc_scr[...] = acc
    l_scr[...] = l

    @pl.when(j == nj - 1)
    def _():
        kn = kn_ref[...].reshape(t_new * heads, LANES).astype(BF16)
        vn = vn_ref[...].reshape(t_new * heads, LANES).astype(BF16)
        x = _dot_nt(w, kn) + bias(t_new * heads, True)
        m_fin = jnp.maximum(m_new, jnp.max(x, axis=1, keepdims=True))
        p = jnp.exp2(x - m_fin)
        beta = jnp.exp2(m_new - m_fin)
        acc2 = beta * acc + jnp.dot(p.astype(BF16), vn, preferred_element_type=F32)
        o = acc2 / (beta * l + jnp.sum(p, axis=1, keepdims=True))
        lam = _diff_lambda(lq1_ref, lk1_ref, lq2_ref, lk2_ref, lam_init)
        for h in range(heads):
            r0 = h * 2 * t_new
            oh = o[r0:r0 + t_new] - lam * o[r0 + t_new:r0 + 2 * t_new]
            gate = g_ref[:, h * LANES:(h + 1) * LANES]
            o_ref[:, h * LANES:(h + 1) * LANES] = _subln_gate(oh, sg_ref, gate, lam_init)


def _attn_decode(page_table, q, kn, vn, ug, sg, lq1, lk1, lq2, lk2, cache_k, cache_v, *,
                 layer, lam_init, pages_per_step=16):
    batch, n_pages = page_table.shape
    _, _, page, heads, _ = cache_k.shape
    t_new = q.shape[0] // batch
    width = q.shape[1]
    gp = pages_per_step
    rows = heads * 2 * t_new
    cols = page * heads
    small = lambda b, j, pt: (0, 0)

    def page_spec(gi):
        return pl.BlockSpec((None, None, page, heads, LANES),
                            lambda b, j, pt: (layer, pt[b, j * gp + gi], 0, 0, 0))

    in_specs = [
        pl.BlockSpec((t_new, width), lambda b, j, pt: (b, 0)),
        pl.BlockSpec((t_new, heads, LANES), lambda b, j, pt: (b, 0, 0)),
        pl.BlockSpec((t_new, heads, LANES), lambda b, j, pt: (b, 0, 0)),
        pl.BlockSpec((t_new, width), lambda b, j, pt: (b, 2)),
        pl.BlockSpec(sg.shape, small),
        pl.BlockSpec(lq1.shape, small), pl.BlockSpec(lk1.shape, small),
        pl.BlockSpec(lq2.shape, small), pl.BlockSpec(lk2.shape, small),
    ] + [page_spec(gi) for gi in range(gp)] * 2
    grid_spec = pltpu.PrefetchScalarGridSpec(
        num_scalar_prefetch=1,
        grid=(batch, n_pages // gp),
        in_specs=in_specs,
        out_specs=pl.BlockSpec((t_new, width), lambda b, j, pt: (b, 0)),
        scratch_shapes=[pltpu.VMEM((rows, LANES), BF16), pltpu.VMEM((rows, cols), F32),
                        pltpu.VMEM((gp, rows, cols), F32), pltpu.VMEM((rows, 1), F32),
                        pltpu.VMEM((rows, 1), F32), pltpu.VMEM((rows, DA_VDIM), F32)],
    )
    page_bytes = page * heads * LANES * 4
    return pl.pallas_call(
        functools.partial(_decode_body, pages_per_step=gp, past_len=n_pages * page, lam_init=lam_init),
        grid_spec=grid_spec,
        out_shape=jax.ShapeDtypeStruct(q.shape, F32),
        compiler_params=pltpu.CompilerParams(
            dimension_semantics=("arbitrary", "arbitrary"),
            vmem_limit_bytes=_vmem_limit(4 * gp * page_bytes + (gp + 4) * rows * cols * 4)),
        name="attn_decode",
    )(page_table, q, kn, vn, ug, sg, lq1, lk1, lq2, lk2, *([cache_k] * gp), *([cache_v] * gp))


def _out_body(mrg_ref, mda_ref, w_ref, x_ref, g_ref, b_ref, of_ref, ob_ref, *, alpha):
    half = mrg_ref.shape[1]
    y = jnp.dot(mrg_ref[...].astype(BF16), w_ref[:half, :], preferred_element_type=F32)
    y = y + jnp.dot(mda_ref[...].astype(BF16), w_ref[half:, :], preferred_element_type=F32)
    z = alpha * x_ref[...] + y
    mu = jnp.mean(z, axis=-1, keepdims=True)
    zc = z - mu
    var = jnp.mean(zc * zc, axis=-1, keepdims=True)
    out = zc * lax.rsqrt(var + LN_EPS) * g_ref[...] + b_ref[...]
    of_ref[...] = out
    ob_ref[...] = out.astype(BF16)


def _out_proj(mrg, mda, w, x, g, b, *, layer, alpha, bm):
    n, d = x.shape
    half = mrg.shape[1]
    rowb = lambda i: (i, 0)
    const = lambda i: (0, 0)
    nbytes = 2 * (2 * bm * half * 4 + w[0].size * 2 + bm * d * 4 + bm * d * 6)
    return pl.pallas_call(
        functools.partial(_out_body, alpha=alpha),
        grid=(n // bm,),
        in_specs=[pl.BlockSpec((bm, half), rowb), pl.BlockSpec((bm, half), rowb),
                  pl.BlockSpec((None,) + w.shape[1:], lambda i: (layer, 0, 0)),
                  pl.BlockSpec((bm, d), rowb),
                  pl.BlockSpec(g.shape, const), pl.BlockSpec(b.shape, const)],
        out_specs=[pl.BlockSpec((bm, d), rowb), pl.BlockSpec((bm, d), rowb)],
        out_shape=[jax.ShapeDtypeStruct((n, d), F32), jax.ShapeDtypeStruct((n, d), BF16)],
        compiler_params=pltpu.CompilerParams(
            dimension_semantics=("arbitrary",),
            vmem_limit_bytes=_vmem_limit(nbytes + 4 * bm * d * 4)),
        name="out_proj",
    )(mrg, mda, w, x, g, b)


def _gate_weights(w_a, w_x):
    per = GATE_TILE // w_a.shape[1]

    def tiles(w):
        w4 = w.reshape(-1, per, w.shape[1], w.shape[2])
        eye = jnp.eye(per, dtype=w.dtype)
        return jnp.einsum("gncd,nm->gncmd", w4, eye).reshape(-1, GATE_TILE, GATE_TILE)

    return jnp.concatenate([tiles(w_a), tiles(w_x)], axis=2).astype(BF16)


def kernel(x_prompt, x_sample, cache_k, cache_v, state_h, state_conv, page_table, w_in, conv_w, conv_b,
           w_ga, b_ga, w_gx, b_gx, lru_lam, lq1, lk1, lq2, lk2, subln_g, w_out, ln_g, ln_b):
    depth = w_in.shape[0]
    bp, seq, d_model = x_prompt.shape
    bs, t_new, _ = x_sample.shape
    heads = cache_k.shape[3]
    d_rg = conv_w.shape[2]
    alpha = (2 * depth) ** 0.25
    slopes = jnp.exp2(-8.0 * jnp.arange(1, heads + 1, dtype=F32) / heads)

    xp = x_prompt.reshape(bp * seq, d_model)
    xs = x_sample.reshape(bs * t_new, d_model)
    xp_b, xs_b = xp.astype(BF16), xs.astype(BF16)
    w_out_b = w_out.astype(BF16)
    zero_h = jnp.zeros((bp, 1, d_rg), F32)
    zero_c = jnp.zeros((bp, CONV_W - 1, d_rg), F32)

    kbig = vbig = None
    ks_l, vs_l, hp_l, cp_l, hs_l, cs_l = [], [], [], [], [], []
    for l in range(depth):
        lam_init = _lambda_init(l)
        ug_p, ug_s = _proj(xp_b, xs_b, w_in, col_blocks=(0, 1, 5), mode="f32", layer=l)
        q_p, q_s = _proj(xp_b, xs_b, w_in, col_blocks=(2,), mode="q", layer=l)
        k_p, kbig, k_s = _proj(xp_b, xs_b, w_in, col_blocks=(3,), mode="kv", big=kbig, layer=l)
        v_p, vbig, v_s = _proj(xp_b, xs_b, w_in, col_blocks=(4,), mode="kv", big=vbig, layer=l)

        wg = _gate_weights(w_ga[l], w_gx[l])
        row = lambda a: a.reshape(1, -1)
        rg_args = (conv_w[l], row(conv_b[l]), wg, row(b_ga[l]), row(b_gx[l]), row(lru_lam[l]))
        mrg_p, ht_p, ct_p = _rglru(ug_p, *rg_args, zero_h, zero_c,
                                   batch=bp, seq=seq, bt=256, out_dtype=BF16)
        mrg_s, ht_s, ct_s = _rglru(ug_s, *rg_args, state_h[l].reshape(bs, 1, d_rg), state_conv[l],
                                   batch=bs, seq=t_new, bt=t_new, out_dtype=F32)

        da_args = (row(subln_g[l]), row(lq1[l]), row(lk1[l]), row(lq2[l]), row(lk2[l]))
        mda_p = _attn_prompt(slopes, q_p, k_p, v_p, ug_p, *da_args,
                             batch=bp, seq=seq, heads=heads, lam_init=lam_init)
        k_s3 = k_s.reshape(bs * t_new, heads, LANES)
        v_s3 = v_s.reshape(bs * t_new, heads, LANES)
        mda_s = _attn_decode(page_table, q_s, k_s3, v_s3, ug_s, *da_args, cache_k, cache_v,
                             layer=l, lam_init=lam_init)

        ln = (row(ln_g[l]), row(ln_b[l]))
        xp, xp_b = _out_proj(mrg_p, mda_p, w_out_b, xp, *ln, layer=l, alpha=alpha, bm=512)
        xs, xs_b = _out_proj(mrg_s, mda_s, w_out_b, xs, *ln, layer=l, alpha=alpha, bm=bs * t_new)

        ks_l.append(k_s3.reshape(bs, t_new, heads, LANES))
        vs_l.append(v_s3.reshape(bs, t_new, heads, LANES))
        hp_l.append(ht_p.reshape(bp, d_rg)); cp_l.append(ct_p)
        hs_l.append(ht_s.reshape(bs, d_rg)); cs_l.append(ct_s)

    return (xp.reshape(bp, seq, d_model), xs.reshape(bs, t_new, d_model),
            kbig.reshape(depth, bp, seq, heads, LANES), vbig.reshape(depth, bp, seq, heads, LANES),
            jnp.stack(hp_l), jnp.stack(cp_l),
            jnp.stack(ks_l), jnp.stack(vs_l), jnp.stack(hs_l), jnp.stack(cs_l))
```

```python
import functools
import math

import jax
import jax.numpy as jnp
from jax import lax
from jax.experimental import pallas as pl
from jax.experimental.pallas import tpu as pltpu

F32 = jnp.float32
BF16 = jnp.bfloat16

H_RG = 16
CONV_W = 4
LRU_C = 8.0
DA_HEAD = 64
DA_VDIM = 2 * DA_HEAD
LN_EPS = 1e-5
SUBLN_EPS = 1e-5
MASKED = -1e30
LOG2_E = math.log2(math.e)
Q_SCALE_LOG2 = DA_HEAD ** -0.5 * LOG2_E
ONES_ROWS = 16

LANES = 128
SUBLANES = 8
MXU_TILE = 256
GATE_TILE = MXU_TILE
V7X_VMEM_BYTES = 64 * 1024 * 1024


def _vmem_limit(nbytes):
    return int(min(V7X_VMEM_BYTES - (8 << 20), nbytes + (16 << 20)))


def _lambda_init(layer):
    return 0.8 - 0.6 * math.exp(-0.3 * layer)


def _diff_lambda(lq1_ref, lk1_ref, lq2_ref, lk2_ref, lam_init):
    a = jnp.sum(lq1_ref[...] * lk1_ref[...], axis=1, keepdims=True)
    b = jnp.sum(lq2_ref[...] * lk2_ref[...], axis=1, keepdims=True)
    return jnp.exp(a) - jnp.exp(b) + lam_init


def _sigmoid(x):
    return 0.5 * jnp.tanh(0.5 * x) + 0.5


def _silu(x):
    hx = 0.5 * x
    return hx * jnp.tanh(hx) + hx


def _fold_rows(x, op, chains=4):
    slabs = [x[i:i + SUBLANES] for i in range(0, x.shape[0], SUBLANES)]
    accs = slabs[:chains]
    for i, s in enumerate(slabs[chains:]):
        accs[i % chains] = op(accs[i % chains], s)
    while len(accs) > 1:
        accs = [op(accs[i], accs[i + 1]) for i in range(0, len(accs), 2)]
    return accs[0]


def _dot_nt(a, b):
    return lax.dot_general(a, b, (((1,), (1,)), ((), ())), preferred_element_type=F32)


def _proj_body(*refs, mode, aliased):
    xp_ref, xs_ref, w_ref = refs[:3]
    outs = refs[3 + (1 if aliased else 0):-1]
    wb_scr = refs[-1]
    os_ref = outs[-1]

    @pl.when(pl.program_id(1) == 0)
    def _():
        wb_scr[...] = w_ref[...].astype(BF16)
        os_ref[...] = jnp.dot(xs_ref[...], wb_scr[...], preferred_element_type=F32)

    acc = jnp.dot(xp_ref[...], wb_scr[...], preferred_element_type=F32)
    if mode == "f32":
        op_ref, os_ref = outs
        op_ref[...] = acc
    elif mode == "q":
        op_ref, os_ref = outs
        op_ref[...] = (acc * Q_SCALE_LOG2).astype(BF16)
    else:
        op_ref, o3_ref, os_ref = outs
        op_ref[...] = acc.astype(BF16)
        for h in range(o3_ref.shape[1]):
            o3_ref[:, h, :] = acc[:, h * LANES:(h + 1) * LANES]


def _proj(xp, xs, w, *, col_blocks, mode, big=None, layer=0, bm=1024, bn=1024):
    depth = w.shape[0]
    n_p, d = xp.shape
    n_s = xs.shape[0]
    ncol = len(col_blocks)
    if col_blocks == (0, 1, 5):
        wcol = lambda j: j + 3 * (j // 2)
    else:
        assert ncol == 1
        c0 = col_blocks[0]
        wcol = lambda j: c0
    in_specs = [
        pl.BlockSpec((bm, d), lambda j, i: (i, 0)),
        pl.BlockSpec((n_s, d), lambda j, i: (0, 0)),
        pl.BlockSpec((None, d, bn), lambda j, i: (layer, 0, wcol(j))),
    ]
    args = [xp, xs, w]
    aliases = {}
    nbytes = 2 * (bm * d * 2 + n_s * d * 2 + d * bn * 4 + n_s * bn * 4) + d * bn * 2
    if mode == "kv":
        heads = bn // LANES
        out_shape = [jax.ShapeDtypeStruct((n_p, bn), BF16),
                     jax.ShapeDtypeStruct((depth, n_p, heads, LANES), F32),
                     jax.ShapeDtypeStruct((n_s, bn), F32)]
        out_specs = [pl.BlockSpec((bm, bn), lambda j, i: (i, 0)),
                     pl.BlockSpec((None, bm, heads, LANES), lambda j, i: (layer, i, 0, 0)),
                     pl.BlockSpec((n_s, bn), lambda j, i: (0, 0))]
        nbytes += 2 * (bm * bn * 2 + bm * bn * 4)
        if big is not None:
            in_specs.append(pl.BlockSpec(memory_space=pl.ANY))
            args.append(big)
            aliases = {3: 1}
    else:
        odt = F32 if mode == "f32" else BF16
        out_shape = [jax.ShapeDtypeStruct((n_p, ncol * bn), odt),
                     jax.ShapeDtypeStruct((n_s, ncol * bn), F32)]
        out_specs = [pl.BlockSpec((bm, bn), lambda j, i: (i, j)),
                     pl.BlockSpec((n_s, bn), lambda j, i: (0, j))]
        nbytes += 2 * bm * bn * 4
    return pl.pallas_call(
        functools.partial(_proj_body, mode=mode, aliased=big is not None),
        grid=(ncol, n_p // bm),
        in_specs=in_specs,
        out_specs=out_specs,
        out_shape=out_shape,
        scratch_shapes=[pltpu.VMEM((d, bn), BF16)],
        input_output_aliases=aliases,
        compiler_params=pltpu.CompilerParams(
            dimension_semantics=("arbitrary", "arbitrary"),
            vmem_limit_bytes=_vmem_limit(nbytes + bm * bn * 4)),
        name="proj_" + mode,
    )(*args)


def _rglru_body(u_ref, g_ref, cw_ref, cb_ref, wg_ref, ba_ref, bx_ref, lam_ref, h0_ref, c0_ref,
                o_ref, ht_ref, ct_ref, ext_scr, a_scr, b_scr, hs_scr, car_scr):
    bt, c = u_ref.shape
    t_blk = pl.program_id(1)
    halo = CONV_W - 1

    @pl.when(t_blk == 0)
    def _():
        ext_scr[SUBLANES - halo:SUBLANES, :] = c0_ref[...]
        car_scr[...] = jnp.broadcast_to(h0_ref[...], car_scr.shape)

    u = u_ref[...]
    ext_scr[SUBLANES:SUBLANES + bt, :] = u
    uc = cb_ref[...] + cw_ref[halo:halo + 1, :] * u
    for j in range(halo):
        lag = halo - j
        uc = uc + cw_ref[j:j + 1, :] * ext_scr[SUBLANES - lag:SUBLANES - lag + bt, :]
    tail = ext_scr[SUBLANES + bt - halo:SUBLANES + bt, :]
    ct_ref[...] = tail
    ext_scr[SUBLANES - halo:SUBLANES, :] = tail

    ucb = uc.astype(BF16)
    ngrp = c // GATE_TILE
    pre_a, pre_x = [], []
    for gi in range(ngrp):
        pre = jnp.dot(ucb[:, gi * GATE_TILE:(gi + 1) * GATE_TILE], wg_ref[gi],
                      preferred_element_type=F32)
        pre_a.append(pre[:, :GATE_TILE])
        pre_x.append(pre[:, GATE_TILE:])
    gate_i = _sigmoid(jnp.concatenate(pre_x, axis=1) + bx_ref[...])
    z = -lam_ref[...]
    softplus = jnp.maximum(z, 0.0) + jnp.log1p(jnp.exp(-jnp.abs(z)))
    half = (-0.5 * LRU_C) * softplus
    log_a = half * jnp.tanh(0.5 * (jnp.concatenate(pre_a, axis=1) + ba_ref[...])) + half
    a = jnp.exp(log_a)
    one_minus_a2 = -jnp.tanh(log_a) * (a * a + 1.0)
    a_scr[...] = a
    b_scr[...] = jnp.sqrt(one_minus_a2) * (gate_i * uc)

    row = lax.broadcasted_iota(jnp.int32, (SUBLANES, c), 0)

    def scan_tile(k, carry):
        sl = pl.ds(pl.multiple_of(k * SUBLANES, SUBLANES), SUBLANES)
        av = a_scr[sl, :]
        bv = b_scr[sl, :]
        for s in (1, 2, 4):
            a_sh = jnp.where(row >= s, pltpu.roll(av, s, 0), 1.0)
            b_sh = jnp.where(row >= s, pltpu.roll(bv, s, 0), 0.0)
            bv = bv + av * b_sh
            av = av * a_sh
        h = bv + av * carry
        hs_scr[sl, :] = h
        return jnp.broadcast_to(h[SUBLANES - 1:SUBLANES, :], h.shape)

    carry = lax.fori_loop(0, bt // SUBLANES, scan_tile, car_scr[...])
    car_scr[...] = carry
    ht_ref[...] = carry[0:1, :]
    o_ref[...] = (hs_scr[...] * _silu(g_ref[...])).astype(o_ref.dtype)


def _rglru(ug, cw, cb, wg, ba, bx, lam, h0, c0, *, batch, seq, bt, out_dtype):
    c = cw.shape[1]
    nt = seq // bt
    halo = CONV_W - 1
    row_blk = lambda b, t: (b * nt + t, 0)
    const2 = lambda b, t: (0, 0)
    in_specs = [
        pl.BlockSpec((bt, c), row_blk),
        pl.BlockSpec((bt, c), lambda b, t: (b * nt + t, 1)),
        pl.BlockSpec(cw.shape, const2),
        pl.BlockSpec(cb.shape, const2),
        pl.BlockSpec(wg.shape, lambda b, t: (0, 0, 0)),
        pl.BlockSpec(ba.shape, const2),
        pl.BlockSpec(bx.shape, const2),
        pl.BlockSpec(lam.shape, const2),
        pl.BlockSpec((None, 1, c), lambda b, t: (b, 0, 0)),
        pl.BlockSpec((None, halo, c), lambda b, t: (b, 0, 0)),
    ]
    out_shape = [jax.ShapeDtypeStruct((batch * seq, c), out_dtype),
                 jax.ShapeDtypeStruct((batch, 1, c), F32),
                 jax.ShapeDtypeStruct((batch, halo, c), F32)]
    out_specs = [pl.BlockSpec((bt, c), row_blk),
                 pl.BlockSpec((None, 1, c), lambda b, t: (b, 0, 0)),
                 pl.BlockSpec((None, halo, c), lambda b, t: (b, 0, 0))]
    scratch = [pltpu.VMEM((bt + SUBLANES, c), F32), pltpu.VMEM((bt, c), F32),
               pltpu.VMEM((bt, c), F32), pltpu.VMEM((bt, c), F32),
               pltpu.VMEM((SUBLANES, c), F32)]
    nbytes = 2 * (2 * bt * c * 4 + bt * c * 4) + 4 * bt * c * 4 + wg.size * 2 * 2
    return pl.pallas_call(
        _rglru_body,
        grid=(batch, nt),
        in_specs=in_specs,
        out_specs=out_specs,
        out_shape=out_shape,
        scratch_shapes=scratch,
        compiler_params=pltpu.CompilerParams(
            dimension_semantics=("arbitrary", "arbitrary"),
            vmem_limit_bytes=_vmem_limit(nbytes + 8 * bt * c * 4)),
        name="rglru",
    )(ug, ug, cw, cb, wg, ba, bx, lam, h0, c0)


def _subln_gate(o, sg_ref, gate, lam_init):
    ms = jnp.mean(o * o, axis=-1, keepdims=True)
    o = o * lax.rsqrt(ms + SUBLN_EPS) * sg_ref[...]
    return (o * (1.0 - lam_init)) * _silu(gate)


def _attn_body(slope_ref, q_ref, k_ref, v_ref, g_ref, sg_ref, lq1_ref, lk1_ref, lq2_ref, lk2_ref,
               o_ref, vt_scr, kb_scr, mask_scr, x_scr, xm_scr, m_scr, acc_scr, *, lam_init):
    blk = q_ref.shape[0]
    nk = vt_scr.shape[0]
    dv = v_ref.shape[1]
    b = pl.program_id(0)
    h = pl.program_id(1)
    qi = pl.program_id(2)
    slope = slope_ref[h] * LOG2_E

    @pl.when(jnp.logical_and(jnp.logical_and(b == 0, h == 0), qi == 0))
    def _():
        key = lax.broadcasted_iota(jnp.int32, (blk, blk), 0)
        qry = lax.broadcasted_iota(jnp.int32, (blk, blk), 1)
        mask_scr[...] = jnp.where(key <= qry, 0.0, MASKED)

    @pl.when(qi == 0)
    def _():
        ones = jnp.ones((ONES_ROWS, blk), BF16)
        for j in range(nk):
            vt_scr[j, :dv, :] = v_ref[j * blk:(j + 1) * blk, :].astype(F32).T.astype(BF16)
            vt_scr[j, dv:, :] = ones
        key = lax.broadcasted_iota(jnp.int32, (blk, LANES), 0)
        kb_scr[...] = slope * key.astype(F32)

    q = q_ref[...]
    lane = lax.broadcasted_iota(jnp.int32, q.shape, 1)
    zero = jnp.zeros_like(q)
    q_maps = (jnp.where(lane < DA_HEAD, q, zero), jnp.where(lane >= DA_HEAD, q, zero))

    def parts(ki, n):
        if ki < n:
            return [((0, blk), (0, blk))]
        return [((0, blk // 2), (0, blk // 2)), ((0, blk), (blk // 2, blk))]

    def scores(ki, n):
        for (k0, k1), (q0, q1) in parts(ki, n):
            k = k_ref[ki * blk + k0:ki * blk + k1, :]
            kb = kb_scr[k0:k1, :] + slope * float((ki - n) * blk)
            kb = pltpu.repeat(kb, (q1 - q0) // LANES, axis=1)
            if ki == n:
                kb = kb + mask_scr[k0:k1, q0:q1]
            for c in range(2):
                x = _dot_nt(k, q_maps[c][q0:q1]) + kb
                x_scr[ki % 2, c, k0:k1, q0:q1] = x
                xm_scr[ki % 2, c, :, q0:q1] = jnp.max(_fold_rows(x, jnp.maximum), axis=0, keepdims=True)

    def softmax_pv(ki, n):
        for (k0, k1), (q0, q1) in parts(ki, n):
            vt = vt_scr[ki, :, k0:k1]
            for c in range(2):
                if ki == 0:
                    m_new = xm_scr[ki % 2, c, :, q0:q1]
                else:
                    m = m_scr[c, :, q0:q1]
                    m_new = jnp.maximum(m, xm_scr[ki % 2, c, :, q0:q1])
                p = jnp.exp2(x_scr[ki % 2, c, k0:k1, q0:q1] - m_new)
                pv = jnp.dot(vt, p.astype(BF16), preferred_element_type=F32)
                if ki > 0:
                    pv = jnp.exp2(m - m_new) * acc_scr[c, :, q0:q1] + pv
                acc_scr[c, :, q0:q1] = pv
                m_scr[c, :, q0:q1] = m_new

    for n in range(nk):
        @pl.when(qi == n)
        def _(n=n):
            scores(0, n)
            for i in range(n):
                scores(i + 1, n)
                softmax_pv(i, n)
            softmax_pv(n, n)

    lam = _diff_lambda(lq1_ref, lk1_ref, lq2_ref, lk2_ref, lam_init)
    inv1 = 1.0 / acc_scr[0, dv:dv + 1, :]
    inv2 = lam / acc_scr[1, dv:dv + 1, :]
    ot = acc_scr[0, :dv, :] * inv1 - acc_scr[1, :dv, :] * inv2
    ms = jnp.mean(ot * ot, axis=0, keepdims=True)
    o = (ot * lax.rsqrt(ms + SUBLN_EPS)).T
    o = (o * sg_ref[...]) * (1.0 - lam_init)
    o_ref[...] = (o * _silu(g_ref[...])).astype(o_ref.dtype)


def _attn_prompt(slopes, q, k, v, ug, sg, lq1, lk1, lq2, lk2, *, batch, seq, heads, lam_init, blk=512):
    nq = seq // blk
    gate_col0 = 2 * (q.shape[1] // LANES)
    qmap = lambda b, h, qi: (b * nq + qi, h)
    kmap = lambda b, h, qi: (b, h)
    small = lambda b, h, qi: (0, 0)
    in_specs = [
        pl.BlockSpec(memory_space=pltpu.SMEM),
        pl.BlockSpec((blk, LANES), qmap),
        pl.BlockSpec((seq, LANES), kmap),
        pl.BlockSpec((seq, LANES), kmap),
        pl.BlockSpec((blk, LANES), lambda b, h, qi: (b * nq + qi, gate_col0 + h)),
        pl.BlockSpec(sg.shape, small),
        pl.BlockSpec(lq1.shape, small), pl.BlockSpec(lk1.shape, small),
        pl.BlockSpec(lq2.shape, small), pl.BlockSpec(lk2.shape, small),
    ]
    return pl.pallas_call(
        functools.partial(_attn_body, lam_init=lam_init),
        grid=(batch, heads, nq),
        in_specs=in_specs,
        out_specs=pl.BlockSpec((blk, LANES), qmap),
        out_shape=jax.ShapeDtypeStruct(q.shape, BF16),
        scratch_shapes=[pltpu.VMEM((nq, DA_VDIM + ONES_ROWS, blk), BF16), pltpu.VMEM((blk, LANES), F32),
                        pltpu.VMEM((blk, blk), F32), pltpu.VMEM((2, 2, blk, blk), F32),
                        pltpu.VMEM((2, 2, 1, blk), F32), pltpu.VMEM((2, 1, blk), F32),
                        pltpu.VMEM((2, DA_VDIM + ONES_ROWS, blk), F32)],
        compiler_params=pltpu.CompilerParams(
            dimension_semantics=("arbitrary", "arbitrary", "arbitrary"),
            vmem_limit_bytes=_vmem_limit(16 * blk * blk * 4)),
        name="attn_prompt",
    )(slopes, q, k, v, ug, sg, lq1, lk1, lq2, lk2)


def _fold_lanes(x, op):
    tiles = [x[:, i:i + LANES] for i in range(0, x.shape[1], LANES)]
    while len(tiles) > 1:
        tiles = [op(tiles[i], tiles[i + 1]) for i in range(0, len(tiles), 2)]
    return tiles[0]


def _decode_body(pt_ref, q_ref, kn_ref, vn_ref, g_ref, sg_ref, lq1_ref, lk1_ref, lq2_ref, lk2_ref,
                 *rest, pages_per_step, past_len, lam_init):
    del pt_ref
    gp = pages_per_step
    k_refs = rest[:gp]
    v_refs = rest[gp:2 * gp]
    o_ref = rest[2 * gp]
    w_scr, c_scr, x_scr, m_scr, l_scr, acc_scr = rest[2 * gp + 1:]
    page, heads, dv = v_refs[0].shape
    t_new = q_ref.shape[0]
    rows = heads * 2 * t_new
    cols = page * heads
    b = pl.program_id(0)
    j = pl.program_id(1)
    nj = pl.num_programs(1)

    def row_slope(shape):
        hq = lax.broadcasted_iota(jnp.int32, shape, 0) // (2 * t_new)
        return jnp.exp2(-(hq + 1).astype(F32)) * LOG2_E

    def bias(ncols, causal):
        r = lax.broadcasted_iota(jnp.int32, (rows, ncols), 0)
        c = lax.broadcasted_iota(jnp.int32, (rows, ncols), 1)
        hq, t = r // (2 * t_new), r % t_new
        s, hk = c // heads, c % heads
        keep = hq == hk
        if causal:
            keep = jnp.logical_and(keep, s <= t)
        return jnp.where(keep, row_slope((rows, ncols)) * (s - t).astype(F32), MASKED)

    @pl.when(jnp.logical_and(b == 0, j == 0))
    def _():
        c_scr[...] = bias(cols, False)

    @pl.when(j == 0)
    def _():
        q = q_ref[...]
        lane = lax.broadcasted_iota(jnp.int32, (t_new, LANES), 1)
        parts = []
        for h in range(heads):
            qh = q[:, h * LANES:(h + 1) * LANES] * Q_SCALE_LOG2
            parts.append(jnp.where(lane < DA_HEAD, qh, 0.0))
            parts.append(jnp.where(lane >= DA_HEAD, qh, 0.0))
        w_scr[...] = jnp.concatenate(parts, axis=0).astype(BF16)
        m_scr[...] = jnp.full(m_scr.shape, MASKED, F32)
        l_scr[...] = jnp.zeros(l_scr.shape, F32)
        acc_scr[...] = jnp.zeros(acc_scr.shape, F32)

    w = w_scr[...]
    slope_col = row_slope((rows, 1))
    offs = [slope_col * ((j * gp + gi) * page - past_len).astype(F32) for gi in range(gp)]

    top = None
    for gi in range(gp):
        page_top = None
        for c0 in range(0, cols, MXU_TILE):
            kp = k_refs[gi][c0 // heads:(c0 + MXU_TILE) // heads].reshape(MXU_TILE, LANES).astype(BF16)
            x = _dot_nt(w, kp) + c_scr[:, c0:c0 + MXU_TILE]
            x_scr[gi, :, c0:c0 + MXU_TILE] = x
            tile_top = _fold_lanes(x, jnp.maximum)
            page_top = tile_top if page_top is None else jnp.maximum(page_top, tile_top)
        page_top = page_top + offs[gi]
        top = page_top if top is None else jnp.maximum(top, page_top)
    m_old = m_scr[...]
    m_new = jnp.maximum(m_old, jnp.max(top, axis=1, keepdims=True))
    m_scr[...] = m_new

    pv = None
    psum = None
    for gi in range(gp):
        p = jnp.exp2(x_scr[gi] - (m_new - offs[gi]))
        vp = v_refs[gi][...].reshape(cols, LANES).astype(BF16)
        part = jnp.dot(p.astype(BF16), vp, preferred_element_type=F32)
        part_sum = _fold_lanes(p, jnp.add)
        pv = part if pv is None else pv + part
        psum = part_sum if psum is None else psum + part_sum
    alpha = jnp.exp2(m_old - m_new)
    acc = alpha * acc_scr[...] + pv
    l = alpha * l_scr[...] + jnp.sum(psum, axis=1, keepdims=True)
    acc_scr[...] = acc
    l_scr[...] = l

    @pl.when(j == nj - 1)
    def _():
        kn = kn_ref[...].reshape(t_new * heads, LANES).astype(BF16)
        vn = vn_ref[...].reshape(t_new * heads, LANES).astype(BF16)
        x = _dot_nt(w, kn) + bias(t_new * heads, True)
        m_fin = jnp.maximum(m_new, jnp.max(x, axis=1, keepdims=True))
        p = jnp.exp2(x - m_fin)
        beta = jnp.exp2(m_new - m_fin)
        acc2 = beta * acc + jnp.dot(p.astype(BF16), vn, preferred_element_type=F32)
        o = acc2 / (beta * l + jnp.sum(p, axis=1, keepdims=True))
        lam = _diff_lambda(lq1_ref, lk1_ref, lq2_ref, lk2_ref, lam_init)
        for h in range(heads):
            r0 = h * 2 * t_new
            oh = o[r0:r0 + t_new] - lam * o[r0 + t_new:r0 + 2 * t_new]
            gate = g_ref[:, h * LANES:(h + 1) * LANES]
            o_ref[:, h * LANES:(h + 1) * LANES] = _subln_gate(oh, sg_ref, gate, lam_init)


def _attn_decode(page_table, q, kn, vn, ug, sg, lq1, lk1, lq2, lk2, cache_k, cache_v, *,
                 layer, lam_init, pages_per_step=16):
    batch, n_pages = page_table.shape
    _, _, page, heads, _ = cache_k.shape
    t_new = q.shape[0] // batch
    width = q.shape[1]
    gp = pages_per_step
    rows = heads * 2 * t_new
    cols = page * heads
    small = lambda b, j, pt: (0, 0)

    def page_spec(gi):
        return pl.BlockSpec((None, None, page, heads, LANES),
                            lambda b, j, pt: (layer, pt[b, j * gp + gi], 0, 0, 0))

    in_specs = [
        pl.BlockSpec((t_new, width), lambda b, j, pt: (b, 0)),
        pl.BlockSpec((t_new, heads, LANES), lambda b, j, pt: (b, 0, 0)),
        pl.BlockSpec((t_new, heads, LANES), lambda b, j, pt: (b, 0, 0)),
        pl.BlockSpec((t_new, width), lambda b, j, pt: (b, 2)),
        pl.BlockSpec(sg.shape, small),
        pl.BlockSpec(lq1.shape, small), pl.BlockSpec(lk1.shape, small),
        pl.BlockSpec(lq2.shape, small), pl.BlockSpec(lk2.shape, small),
    ] + [page_spec(gi) for gi in range(gp)] * 2
    grid_spec = pltpu.PrefetchScalarGridSpec(
        num_scalar_prefetch=1,
        grid=(batch, n_pages // gp),
        in_specs=in_specs,
        out_specs=pl.BlockSpec((t_new, width), lambda b, j, pt: (b, 0)),
        scratch_shapes=[pltpu.VMEM((rows, LANES), BF16), pltpu.VMEM((rows, cols), F32),
                        pltpu.VMEM((gp, rows, cols), F32), pltpu.VMEM((rows, 1), F32),
                        pltpu.VMEM((rows, 1), F32), pltpu.VMEM((rows, DA_VDIM), F32)],
    )
    page_bytes = page * heads * LANES * 4
    return pl.pallas_call(
        functools.partial(_decode_body, pages_per_step=gp, past_len=n_pages * page, lam_init=lam_init),
        grid_spec=grid_spec,
        out_shape=jax.ShapeDtypeStruct(q.shape, F32),
        compiler_params=pltpu.CompilerParams(
            dimension_semantics=("arbitrary", "arbitrary"),
            vmem_limit_bytes=_vmem_limit(4 * gp * page_bytes + (gp + 4) * rows * cols * 4)),
        name="attn_decode",
    )(page_table, q, kn, vn, ug, sg, lq1, lk1, lq2, lk2, *([cache_k] * gp), *([cache_v] * gp))


def _out_body(mrg_ref, mda_ref, w_ref, x_ref, g_ref, b_ref, of_ref, ob_ref, *, alpha):
    half = mrg_ref.shape[1]
    y = jnp.dot(mrg_ref[...].astype(BF16), w_ref[:half, :], preferred_element_type=F32)
    y = y + jnp.dot(mda_ref[...].astype(BF16), w_ref[half:, :], preferred_element_type=F32)
    z = alpha * x_ref[...] + y
    mu = jnp.mean(z, axis=-1, keepdims=True)
    zc = z - mu
    var = jnp.mean(zc * zc, axis=-1, keepdims=True)
    out = zc * lax.rsqrt(var + LN_EPS) * g_ref[...] + b_ref[...]
    of_ref[...] = out
    ob_ref[...] = out.astype(BF16)


def _out_proj(mrg, mda, w, x, g, b, *, layer, alpha, bm):
    n, d = x.shape
    half = mrg.shape[1]
    rowb = lambda i: (i, 0)
    const = lambda i: (0, 0)
    nbytes = 2 * (2 * bm * half * 4 + w[0].size * 2 + bm * d * 4 + bm * d * 6)
    return pl.pallas_call(
        functools.partial(_out_body, alpha=alpha),
        grid=(n // bm,),
        in_specs=[pl.BlockSpec((bm, half), rowb), pl.BlockSpec((bm, half), rowb),
                  pl.BlockSpec((None,) + w.shape[1:], lambda i: (layer, 0, 0)),
                  pl.BlockSpec((bm, d), rowb),
                  pl.BlockSpec(g.shape, const), pl.BlockSpec(b.shape, const)],
        out_specs=[pl.BlockSpec((bm, d), rowb), pl.BlockSpec((bm, d), rowb)],
        out_shape=[jax.ShapeDtypeStruct((n, d), F32), jax.ShapeDtypeStruct((n, d), BF16)],
        compiler_params=pltpu.CompilerParams(
            dimension_semantics=("arbitrary",),
            vmem_limit_bytes=_vmem_limit(nbytes + 4 * bm * d * 4)),
        name="out_proj",
    )(mrg, mda, w, x, g, b)


def _gate_weights(w_a, w_x):
    per = GATE_TILE // w_a.shape[1]

    def tiles(w):
        w4 = w.reshape(-1, per, w.shape[1], w.shape[2])
        eye = jnp.eye(per, dtype=w.dtype)
        return jnp.einsum("gncd,nm->gncmd", w4, eye).reshape(-1, GATE_TILE, GATE_TILE)

    return jnp.concatenate([tiles(w_a), tiles(w_x)], axis=2).astype(BF16)


def kernel(x_prompt, x_sample, cache_k, cache_v, state_h, state_conv, page_table, w_in, conv_w, conv_b,
           w_ga, b_ga, w_gx, b_gx, lru_lam, lq1, lk1, lq2, lk2, subln_g, w_out, ln_g, ln_b):
    depth = w_in.shape[0]
    bp, seq, d_model = x_prompt.shape
    bs, t_new, _ = x_sample.shape
    heads = cache_k.shape[3]
    d_rg = conv_w.shape[2]
    alpha = (2 * depth) ** 0.25
    slopes = jnp.exp2(-8.0 * jnp.arange(1, heads + 1, dtype=F32) / heads)

    xp = x_prompt.reshape(bp * seq, d_model)
    xs = x_sample.reshape(bs * t_new, d_model)
    xp_b, xs_b = xp.astype(BF16), xs.astype(BF16)
    w_out_b = w_out.astype(BF16)
    zero_h = jnp.zeros((bp, 1, d_rg), F32)
    zero_c = jnp.zeros((bp, CONV_W - 1, d_rg), F32)

    kbig = vbig = None
    ks_l, vs_l, hp_l, cp_l, hs_l, cs_l = [], [], [], [], [], []
    for l in range(depth):
        lam_init = _lambda_init(l)
        ug_p, ug_s = _proj(xp_b, xs_b, w_in, col_blocks=(0, 1, 5), mode="f32", layer=l)
        q_p, q_s = _proj(xp_b, xs_b, w_in, col_blocks=(2,), mode="q", layer=l)
        k_p, kbig, k_s = _proj(xp_b, xs_b, w_in, col_blocks=(3,), mode="kv", big=kbig, layer=l)
        v_p, vbig, v_s = _proj(xp_b, xs_b, w_in, col_blocks=(4,), mode="kv", big=vbig, layer=l)

        wg = _gate_weights(w_ga[l], w_gx[l])
        row = lambda a: a.reshape(1, -1)
        rg_args = (conv_w[l], row(conv_b[l]), wg, row(b_ga[l]), row(b_gx[l]), row(lru_lam[l]))
        mrg_p, ht_p, ct_p = _rglru(ug_p, *rg_args, zero_h, zero_c,
                                   batch=bp, seq=seq, bt=256, out_dtype=BF16)
        mrg_s, ht_s, ct_s = _rglru(ug_s, *rg_args, state_h[l].reshape(bs, 1, d_rg), state_conv[l],
                                   batch=bs, seq=t_new, bt=t_new, out_dtype=F32)

        da_args = (row(subln_g[l]), row(lq1[l]), row(lk1[l]), row(lq2[l]), row(lk2[l]))
        mda_p = _attn_prompt(slopes, q_p, k_p, v_p, ug_p, *da_args,
                             batch=bp, seq=seq, heads=heads, lam_init=lam_init)
        k_s3 = k_s.reshape(bs * t_new, heads, LANES)
        v_s3 = v_s.reshape(bs * t_new, heads, LANES)
        mda_s = _attn_decode(page_table, q_s, k_s3, v_s3, ug_s, *da_args, cache_k, cache_v,
                             layer=l, lam_init=lam_init)

        ln = (row(ln_g[l]), row(ln_b[l]))
        xp, xp_b = _out_proj(mrg_p, mda_p, w_out_b, xp, *ln, layer=l, alpha=alpha, bm=512)
        xs, xs_b = _out_proj(mrg_s, mda_s, w_out_b, xs, *ln, layer=l, alpha=alpha, bm=bs * t_new)

        ks_l.append(k_s3.reshape(bs, t_new, heads, LANES))
        vs_l.append(v_s3.reshape(bs, t_new, heads, LANES))
        hp_l.append(ht_p.reshape(bp, d_rg)); cp_l.append(ct_p)
        hs_l.append(ht_s.reshape(bs, d_rg)); cs_l.append(ct_s)

    return (xp.reshape(bp, seq, d_model), xs.reshape(bs, t_new, d_model),
            kbig.reshape(depth, bp, seq, heads, LANES), vbig.reshape(depth, bp, seq, heads, LANES),
            jnp.stack(hp_l), jnp.stack(cp_l),
            jnp.stack(ks_l), jnp.stack(vs_l), jnp.stack(hs_l), jnp.stack(cs_l))
```
